```python
import math
import jax, jax.numpy as jnp
from jax import lax
import numpy as np

D_MODEL = 2048
BATCH = 8
SEQ = 2048
DEPTH = 2
DEC_BATCH = 128
DEC_SEQ = 8
PAST_LEN = 2048
PAGE_SIZE = 128

MIX_WIDTH = D_MODEL
ATTN_WIDTH = MIX_WIDTH // 2
SSM_WIDTH = MIX_WIDTH - ATTN_WIDTH
N_HEADS = 8
DK = ATTN_WIDTH // (2 * N_HEADS)
VD = 2 * DK
IN_WIDTH = 3 * ATTN_WIDTH + SSM_WIDTH
SSM_GROUP = 16
SSM_GROUPS = SSM_WIDTH // SSM_GROUP
SSM_STATE = 64
N_BUCKETS = 32
MAX_DISTANCE = 128
QBLOCK = 128
PEER_HEADS = 8
PEER_NKEYS = 128
PEER_NEXP = PEER_NKEYS * PEER_NKEYS
PEER_TOPK = 16
PEER_QDIM = 256
PEER_HALF = PEER_QDIM // 2
PEER_CHUNK = 128
PLE_DIM = 256
RMS_EPS = 1e-6
NEG_INF = -1e30

kernel_name = 'hymba_s5_diffattn_peer_step'


def rms_norm(x, g):
    x32 = x.astype(jnp.float32)
    y = x32 * lax.rsqrt(jnp.mean(x32 * x32, axis=-1, keepdims=True) + RMS_EPS)
    return (y * g.astype(jnp.float32)).astype(x.dtype)


def rel_bucket(rel):
    n = jnp.maximum(rel, 0)
    max_exact = N_BUCKETS // 2
    nf = jnp.maximum(n, max_exact).astype(jnp.float32)
    large = max_exact + (jnp.log(nf / max_exact) / math.log(MAX_DISTANCE / max_exact)
                         * (N_BUCKETS - max_exact)).astype(jnp.int32)
    large = jnp.minimum(large, N_BUCKETS - 1)
    return jnp.where(n < max_exact, n, large)


def diff_attention(q, k, v, q_pos, k_pos, rel_bias, lam, subln_g, lam_init):
    bsz, tq = q.shape[:2]
    qb = QBLOCK if tq % QBLOCK == 0 else tq
    nb = tq // qb
    k32 = k.astype(jnp.float32)
    v32 = v.astype(jnp.float32)
    scale = DK ** -0.5
    g32 = subln_g.astype(jnp.float32)

    def block(args):
        qblk, pblk = args
        s = jnp.einsum('bqhmd,bkhmd->bmhqk', qblk.astype(jnp.float32), k32) * scale
        rel = pblk[:, None] - k_pos[None, :]
        bias = jnp.moveaxis(rel_bias[rel_bucket(rel)].astype(jnp.float32), -1, 0)
        s = jnp.where(rel >= 0, s + bias, NEG_INF)
        p = jax.nn.softmax(s, axis=-1)
        w = p[:, 0] - lam * p[:, 1]
        o = jnp.einsum('bhqk,bkhe->bqhe', w, v32)
        o = o * lax.rsqrt(jnp.mean(o * o, axis=-1, keepdims=True) + RMS_EPS)
        return o * g32 * (1.0 - lam_init)

    qs = jnp.moveaxis(q.reshape(bsz, nb, qb, N_HEADS, 2, DK), 1, 0)
    ps = q_pos.reshape(nb, qb)
    o = lax.map(block, (qs, ps))
    return jnp.moveaxis(o, 0, 1).reshape(bsz, tq, N_HEADS * VD)


def _ssm_combine(c1, c2):
    a1, b1 = c1
    a2, b2 = c2
    return a1 * a2, a2 * b1 + b2


def ssm_mix(u, h0, a_re, a_im, b_re, b_im, c_re, c_im, d, log_dt, w_glu, b_glu):
    f32 = jnp.float32
    bsz, t = u.shape[:2]
    a = lax.complex(a_re.astype(f32), a_im.astype(f32))
    dt = jnp.exp(log_dt.astype(f32))[:, None]
    a_bar = jnp.exp(a * dt)
    b_bar = ((a_bar - 1.0) / a)[:, :, None] * lax.complex(b_re.astype(f32), b_im.astype(f32))
    c = lax.complex(c_re.astype(f32), c_im.astype(f32))
    u32 = u.astype(f32).reshape(bsz, t, SSM_GROUPS, SSM_GROUP)
    bu = jnp.einsum('gnp,btgp->tbgn', b_bar, u32.astype(jnp.complex64))
    bu = bu.at[0].add(a_bar[None] * h0)
    a_el = jnp.broadcast_to(a_bar[None, None], (t, 1, SSM_GROUPS, SSM_STATE))
    _, h = lax.associative_scan(_ssm_combine, (a_el, bu), axis=0)
    y = jnp.einsum('gpn,tbgn->btgp', c, h).real + d.astype(f32).reshape(SSM_GROUPS, SSM_GROUP) * u32
    z = jax.nn.gelu(y.reshape(bsz, t, SSM_WIDTH))
    out = z * jax.nn.sigmoid(z @ w_glu.astype(f32) + b_glu.astype(f32))
    return out.astype(u.dtype), h[-1]


def peer(x, w_q, keys, u_tab, v_tab):
    bsz, t, dm = x.shape
    n = bsz * t
    pad = (-n) % PEER_CHUNK
    xc = jnp.pad(x.reshape(n, dm), ((0, pad), (0, 0))).reshape(-1, PEER_CHUNK, dm)
    keys32 = keys.astype(jnp.float32)

    def chunk(xb):
        q = (xb @ w_q).astype(jnp.float32).reshape(PEER_CHUNK, PEER_HEADS, 2, PEER_HALF)
        s = jnp.einsum('chmd,mnd->chmn', q, keys32)
        sa, ia = lax.top_k(s[:, :, 0], PEER_TOPK)
        sb, ib = lax.top_k(s[:, :, 1], PEER_TOPK)
        cand = (sa[..., :, None] + sb[..., None, :]).reshape(PEER_CHUNK, PEER_HEADS, PEER_TOPK * PEER_TOPK)
        cidx = (ia[..., :, None] * PEER_NKEYS + ib[..., None, :]).reshape(PEER_CHUNK, PEER_HEADS, PEER_TOPK * PEER_TOPK)
        st, pos = lax.top_k(cand, PEER_TOPK)
        eidx = jnp.take_along_axis(cidx, pos, axis=-1)
        g = jax.nn.softmax(st, axis=-1)
        u_e = u_tab[eidx]
        v_e = v_tab[eidx]
        act = jax.nn.gelu(jnp.einsum('chkd,cd->chk', u_e, xb).astype(jnp.float32))
        return jnp.einsum('chk,chkd->cd', (g * act).astype(v_e.dtype), v_e)

    y = lax.map(chunk, xc).reshape(-1, dm)[:n]
    return y.reshape(bsz, t, dm).astype(x.dtype)


def trunk_layer(i, x, ple, h0, k_past, v_past, q_pos, k_pos, weights):
    (norm_mix_g, norm_ffn_g, w_in, w_out, lam_q1, lam_k1, lam_q2, lam_k2, subln_g, rel_bias,
     ssm_a_re, ssm_a_im, ssm_b_re, ssm_b_im, ssm_c_re, ssm_c_im, ssm_d, ssm_log_dt, ssm_w_glu, ssm_b_glu,
     peer_w_q, peer_keys, peer_u, peer_v, ple_w, ple_gate_w) = weights
    f32 = jnp.float32
    bsz, t = x.shape[:2]
    xn = rms_norm(x, norm_mix_g[i])
    proj = xn @ w_in[i]
    q = proj[..., :ATTN_WIDTH].reshape(bsz, t, N_HEADS, 2, DK)
    k_new = proj[..., ATTN_WIDTH:2 * ATTN_WIDTH].reshape(bsz, t, N_HEADS, 2 * DK)
    v_new = proj[..., 2 * ATTN_WIDTH:3 * ATTN_WIDTH].reshape(bsz, t, N_HEADS, VD)
    u = proj[..., 3 * ATTN_WIDTH:]
    if k_past is None:
        k_all, v_all = k_new, v_new
    else:
        k_all = jnp.concatenate([k_past.astype(k_new.dtype), k_new], axis=1)
        v_all = jnp.concatenate([v_past.astype(v_new.dtype), v_new], axis=1)
    lam_init = 0.8 - 0.6 * math.exp(-0.3 * i)
    lam = (jnp.exp(jnp.sum(lam_q1[i].astype(f32) * lam_k1[i].astype(f32)))
           - jnp.exp(jnp.sum(lam_q2[i].astype(f32) * lam_k2[i].astype(f32))) + lam_init)
    attn = diff_attention(q, k_all.reshape(bsz, -1, N_HEADS, 2, DK), v_all, q_pos, k_pos,
                          rel_bias, lam, subln_g[i], lam_init)
    ssm_out, h_last = ssm_mix(u, h0, ssm_a_re[i], ssm_a_im[i], ssm_b_re[i], ssm_b_im[i],
                              ssm_c_re[i], ssm_c_im[i], ssm_d[i], ssm_log_dt[i], ssm_w_glu[i], ssm_b_glu[i])
    mix = jnp.concatenate([attn.astype(x.dtype), ssm_out], axis=-1) @ w_out[i]
    h = x + mix
    h = h + peer(rms_norm(h, norm_ffn_g[i]), peer_w_q[i], peer_keys[i], peer_u[i], peer_v[i])
    h = h + (ple @ ple_w[i]) * jax.nn.sigmoid(h @ ple_gate_w[i])
    return h, k_new, v_new, h_last


def setup_inputs(seed: int = 0) -> dict:
    key = jax.random.key(seed)
    ks = jax.random.split(key, 40)
    f32 = jnp.float32

    def nrm(k, shape, scale):
        return jax.random.normal(k, shape, f32) * scale

    n_pages = PAST_LEN // PAGE_SIZE
    n_used = DEC_BATCH * n_pages
    n_pool = n_used + max(1, n_used // 4)
    page_table = jax.random.permutation(ks[0], n_pool)[:n_used].reshape(DEC_BATCH, n_pages).astype(jnp.int32)
    a_im0 = math.pi * jnp.arange(SSM_STATE, dtype=f32)
    return {
        'x_prompt': nrm(ks[1], (BATCH, SEQ, D_MODEL), 1.0),
        'x_sample': nrm(ks[2], (DEC_BATCH, DEC_SEQ, D_MODEL), 1.0),
        'cache_k': nrm(ks[3], (DEPTH, n_pool, PAGE_SIZE, N_HEADS, 2 * DK), 1.0),
        'cache_v': nrm(ks[4], (DEPTH, n_pool, PAGE_SIZE, N_HEADS, VD), 1.0),
        'state_ssm_re': nrm(ks[5], (DEPTH, DEC_BATCH, SSM_GROUPS, SSM_STATE), 0.5),
        'state_ssm_im': nrm(ks[6], (DEPTH, DEC_BATCH, SSM_GROUPS, SSM_STATE), 0.5),
        'page_table': page_table,
        'p_prompt': nrm(ks[7], (DEPTH, BATCH, SEQ, PLE_DIM), 1.0),
        'p_sample': nrm(ks[8], (DEPTH, DEC_BATCH, DEC_SEQ, PLE_DIM), 1.0),
        'norm_mix_g': 1.0 + nrm(ks[9], (DEPTH, D_MODEL), 0.02),
        'norm_ffn_g': 1.0 + nrm(ks[10], (DEPTH, D_MODEL), 0.02),
        'w_in': nrm(ks[11], (DEPTH, D_MODEL, IN_WIDTH), D_MODEL ** -0.5),
        'w_out': nrm(ks[12], (DEPTH, MIX_WIDTH, D_MODEL), MIX_WIDTH ** -0.5),
        'lam_q1': nrm(ks[13], (DEPTH, DK), 0.1),
        'lam_k1': nrm(ks[14], (DEPTH, DK), 0.1),
        'lam_q2': nrm(ks[15], (DEPTH, DK), 0.1),
        'lam_k2': nrm(ks[16], (DEPTH, DK), 0.1),
        'subln_g': 1.0 + nrm(ks[17], (DEPTH, VD), 0.02),
        'rel_bias': nrm(ks[18], (N_BUCKETS, N_HEADS), 0.1),
        'ssm_a_re': -0.5 + nrm(ks[19], (DEPTH, SSM_GROUPS, SSM_STATE), 0.01),
        'ssm_a_im': a_im0 + nrm(ks[20], (DEPTH, SSM_GROUPS, SSM_STATE), 0.01),
        'ssm_b_re': nrm(ks[21], (DEPTH, SSM_GROUPS, SSM_STATE, SSM_GROUP), (2 * SSM_GROUP) ** -0.5),
        'ssm_b_im': nrm(ks[22], (DEPTH, SSM_GROUPS, SSM_STATE, SSM_GROUP), (2 * SSM_GROUP) ** -0.5),
        'ssm_c_re': nrm(ks[23], (DEPTH, SSM_GROUPS, SSM_GROUP, SSM_STATE), (2 * SSM_STATE) ** -0.5),
        'ssm_c_im': nrm(ks[24], (DEPTH, SSM_GROUPS, SSM_GROUP, SSM_STATE), (2 * SSM_STATE) ** -0.5),
        'ssm_d': nrm(ks[25], (DEPTH, SSM_WIDTH), 0.5),
        'ssm_log_dt': jax.random.uniform(ks[26], (DEPTH, SSM_GROUPS), f32, math.log(1e-3), math.log(1e-1)),
        'ssm_w_glu': nrm(ks[27], (DEPTH, SSM_WIDTH, SSM_WIDTH), SSM_WIDTH ** -0.5),
        'ssm_b_glu': nrm(ks[28], (DEPTH, SSM_WIDTH), 0.01),
        'peer_w_q': nrm(ks[29], (DEPTH, D_MODEL, PEER_HEADS * PEER_QDIM), D_MODEL ** -0.5),
        'peer_keys': nrm(ks[30], (DEPTH, 2, PEER_NKEYS, PEER_HALF), PEER_HALF ** -0.5),
        'peer_u': nrm(ks[31], (DEPTH, PEER_NEXP, D_MODEL), D_MODEL ** -0.5),
        'peer_v': nrm(ks[32], (DEPTH, PEER_NEXP, D_MODEL), (PEER_HEADS * PEER_TOPK) ** -0.5),
        'ple_w': nrm(ks[33], (DEPTH, PLE_DIM, D_MODEL), PLE_DIM ** -0.5),
        'ple_gate_w': nrm(ks[34], (DEPTH, D_MODEL, D_MODEL), D_MODEL ** -0.5),
        'final_norm_g': 1.0 + nrm(ks[35], (D_MODEL,), 0.02),
    }


def reference(x_prompt, x_sample, cache_k, cache_v, state_ssm_re, state_ssm_im, page_table,
              p_prompt, p_sample, norm_mix_g, norm_ffn_g, w_in, w_out, lam_q1, lam_k1, lam_q2, lam_k2,
              subln_g, rel_bias, ssm_a_re, ssm_a_im, ssm_b_re, ssm_b_im, ssm_c_re, ssm_c_im, ssm_d,
              ssm_log_dt, ssm_w_glu, ssm_b_glu, peer_w_q, peer_keys, peer_u, peer_v, ple_w, ple_gate_w,
              final_norm_g):
    f32 = jnp.float32
    weights = (norm_mix_g, norm_ffn_g, w_in, w_out, lam_q1, lam_k1, lam_q2, lam_k2, subln_g, rel_bias,
               ssm_a_re, ssm_a_im, ssm_b_re, ssm_b_im, ssm_c_re, ssm_c_im, ssm_d, ssm_log_dt, ssm_w_glu, ssm_b_glu,
               peer_w_q, peer_keys, peer_u, peer_v, ple_w, ple_gate_w)
    bsz, seq = x_prompt.shape[:2]
    dec_batch, dec_seq = x_sample.shape[:2]
    n_pages = page_table.shape[1]
    past_len = n_pages * cache_k.shape[2]
    pos_prompt = jnp.arange(seq, dtype=jnp.int32)
    q_pos_s = past_len + jnp.arange(dec_seq, dtype=jnp.int32)
    k_pos_s = jnp.arange(past_len + dec_seq, dtype=jnp.int32)

    hp, hs = x_prompt, x_sample
    kp_l, vp_l, sp_l, ks_l, vs_l, ss_l = [], [], [], [], [], []
    for i in range(DEPTH):
        h0p = jnp.zeros((bsz, SSM_GROUPS, SSM_STATE), jnp.complex64)
        hp, kp, vp, sp = trunk_layer(i, hp, p_prompt[i], h0p, None, None, pos_prompt, pos_prompt, weights)
        k_past = cache_k[i][page_table].reshape(dec_batch, past_len, N_HEADS, 2 * DK)
        v_past = cache_v[i][page_table].reshape(dec_batch, past_len, N_HEADS, VD)
        h0s = lax.complex(state_ssm_re[i].astype(f32), state_ssm_im[i].astype(f32))
        hs, ks, vs, ss = trunk_layer(i, hs, p_sample[i], h0s, k_past, v_past, q_pos_s, k_pos_s, weights)
        kp_l.append(kp); vp_l.append(vp); sp_l.append(sp)
        ks_l.append(ks); vs_l.append(vs); ss_l.append(ss)

    y_prompt = rms_norm(hp, final_norm_g)
    y_sample = rms_norm(hs, final_norm_g)
    return (y_prompt, y_sample,
            jnp.stack(kp_l), jnp.stack(vp_l),
            jnp.stack([s.real for s in sp_l]), jnp.stack([s.imag for s in sp_l]),
            jnp.stack(ks_l), jnp.stack(vs_l),
            jnp.stack([s.real for s in ss_l]), jnp.stack([s.imag for s in ss_l]))
```

```python
import functools
import math

import jax
import jax.numpy as jnp
from jax import lax
from jax.experimental import pallas as pl
from jax.experimental.pallas import tpu as pltpu

F32 = jnp.float32
BF16 = jnp.bfloat16

D_MODEL = 2048
DEPTH = 2
N_HEADS = 8
DK = 64
VD = 128
ATTN_WIDTH = N_HEADS * VD
SSM_WIDTH = 1024
SSM_GROUP = 16
SSM_GROUPS = 64
SSM_STATE = 64
N_BUCKETS = 32
MAX_DISTANCE = 128
PEER_HEADS = 8
PEER_NKEYS = 128
PEER_TOPK = 16
PEER_HALF = 128
PLE_DIM = 256
RMS_EPS = 1e-6
NEG_INF = -1e30

LANES = 128
SSM_CHUNK = 8
SSM_PAIR = 2 * SSM_CHUNK * SSM_GROUP
SSM_PAIRS = SSM_GROUPS // 2
SSM_PAIRS_PER_STEP = 4
VMEM_LIMIT = 50 * 1024 * 1024

_PEER_CAND = [(i, j) for i in range(PEER_TOPK) for j in range(PEER_TOPK) if (i + 1) * (j + 1) <= PEER_TOPK]
_PEER_CAND_ROWS = 56


def _params(*sem):
    return pltpu.CompilerParams(dimension_semantics=sem, vmem_limit_bytes=VMEM_LIMIT)


def _gelu(x):
    return 0.5 * x * (1.0 + jnp.tanh(math.sqrt(2.0 / math.pi) * (x + 0.044715 * (x * x * x))))


def _sigmoid(x):
    return 1.0 / (1.0 + jnp.exp(-x))


def _rms(x, g):
    return x * lax.rsqrt(jnp.mean(x * x, axis=-1, keepdims=True) + RMS_EPS) * g


def _in_proj_kernel(x_ref, g_ref, w_ref, q_ref, k_ref, v_ref, u_ref, xn_ref):
    j = pl.program_id(1)

    @pl.when(j == 0)
    def _():
        xn_ref[...] = _rms(x_ref[...], g_ref[...]).astype(BF16)

    acc = jnp.dot(xn_ref[...], w_ref[...], preferred_element_type=F32)

    @pl.when(j == 0)
    def _():
        q_ref[...] = (acc * DK ** -0.5).astype(q_ref.dtype)

    @pl.when(j == 1)
    def _():
        k_ref[...] = acc

    @pl.when(j == 2)
    def _():
        v_ref[...] = acc

    @pl.when(j == 3)
    def _():
        u_ref[...] = acc


def _in_proj(x, g, w_bf16, q_dtype, tm=512):
    m = x.shape[0]
    slab = pl.BlockSpec((tm, ATTN_WIDTH), lambda i, j: (i, 0))
    return pl.pallas_call(
        _in_proj_kernel,
        grid=(m // tm, 4),
        in_specs=[pl.BlockSpec((tm, D_MODEL), lambda i, j: (i, 0)),
                  pl.BlockSpec((1, D_MODEL), lambda i, j: (0, 0)),
                  pl.BlockSpec((D_MODEL, ATTN_WIDTH), lambda i, j: (0, j))],
        out_specs=[slab, slab, slab, slab],
        out_shape=[jax.ShapeDtypeStruct((m, ATTN_WIDTH), q_dtype),
                   jax.ShapeDtypeStruct((m, ATTN_WIDTH), F32),
                   jax.ShapeDtypeStruct((m, ATTN_WIDTH), F32),
                   jax.ShapeDtypeStruct((m, SSM_WIDTH), F32)],
        scratch_shapes=[pltpu.VMEM((tm, D_MODEL), BF16)],
        compiler_params=_params("parallel", "arbitrary"),
        name="in_proj",
    )(x, g.reshape(1, D_MODEL), w_bf16)


def _lam_value(lam_ref, lam_init):
    lp = lam_ref[...]
    l1 = jnp.sum(lp[0:1] * lp[1:2], axis=-1, keepdims=True)
    l2 = jnp.sum(lp[2:3] * lp[3:4], axis=-1, keepdims=True)
    return jnp.exp(l1) - jnp.exp(l2) + lam_init


def _split_maps(q):
    lane = lax.broadcasted_iota(jnp.int32, q.shape, 1)
    zero = jnp.zeros_like(q)
    return jnp.concatenate([jnp.where(lane < DK, q, zero), jnp.where(lane >= DK, q, zero)], axis=0)


def _rel_bias_by_distance(rel_bias, rel):
    n = jnp.maximum(rel, 0)
    max_exact = N_BUCKETS // 2
    nf = jnp.maximum(n, max_exact).astype(F32)
    large = max_exact + (jnp.log(nf / max_exact) / math.log(MAX_DISTANCE / max_exact)
                         * (N_BUCKETS - max_exact)).astype(jnp.int32)
    large = jnp.minimum(large, N_BUCKETS - 1)
    bucket = jnp.where(n < max_exact, n, large)
    bias = jnp.moveaxis(rel_bias[bucket].astype(F32), -1, 0)
    return jnp.where(rel >= 0, bias, NEG_INF)


def _attn_prompt_kernel(lam_ref, q_ref, k_ref, v_ref, bias_ref, g_ref, o_ref, acc_scr, *, tq, lam_init):
    qi = pl.program_id(2)
    qq = _split_maps(q_ref[...])
    acc_scr[...] = jnp.zeros_like(acc_scr)

    def body(kj, carry):
        m_old, l_old = carry
        start = pl.multiple_of(kj * tq, tq)
        kb = k_ref[pl.ds(start, tq), :].astype(BF16)
        vb = v_ref[pl.ds(start, tq), :].astype(BF16)
        s = lax.dot_general(kb, qq, (((1,), (1,)), ((), ())), preferred_element_type=F32)
        bias = bias_ref[0, qi - kj]
        s = s + jnp.concatenate([bias, bias], axis=1)
        m_new = jnp.maximum(m_old, jnp.max(s, axis=0, keepdims=True))
        alpha = jnp.exp(m_old - m_new)
        p = jnp.exp(s - m_new)
        l_new = alpha * l_old + jnp.sum(p, axis=0, keepdims=True)
        pv = lax.dot_general(vb, p.astype(BF16), (((0,), (0,)), ((), ())), preferred_element_type=F32)
        acc_scr[...] = alpha * acc_scr[...] + pv
        return m_new, l_new

    m0 = jnp.full((1, 2 * tq), NEG_INF, F32)
    l0 = jnp.zeros((1, 2 * tq), F32)
    _, l_fin = lax.fori_loop(0, qi + 1, body, (m0, l0))

    lam = _lam_value(lam_ref, lam_init)
    acc = acc_scr[...] / l_fin
    o = acc[:, :tq] - lam * acc[:, tq:]
    o = o * lax.rsqrt(jnp.mean(o * o, axis=0, keepdims=True) + RMS_EPS)
    o = o * g_ref[...] * (1.0 - lam_init)
    o_ref[...] = o.T.astype(o_ref.dtype)


def _attn_prompt(q, k, v, bias_t, lam_params, subln_g, lam_init, bsz, seq, tq=256):
    nq = seq // tq
    kv_spec = pl.BlockSpec((seq, VD), lambda b, h, i: (b, h))
    return pl.pallas_call(
        functools.partial(_attn_prompt_kernel, tq=tq, lam_init=lam_init),
        grid=(bsz, N_HEADS, nq),
        in_specs=[pl.BlockSpec((4, DK), lambda b, h, i: (0, 0)),
                  pl.BlockSpec((tq, VD), lambda b, h, i: (b * nq + i, h)),
                  kv_spec, kv_spec,
                  pl.BlockSpec((1, nq, tq, tq), lambda b, h, i: (h, 0, 0, 0)),
                  pl.BlockSpec((VD, 1), lambda b, h, i: (0, 0))],
        out_specs=pl.BlockSpec((tq, VD), lambda b, h, i: (b * nq + i, h)),
        out_shape=jax.ShapeDtypeStruct((bsz * seq, ATTN_WIDTH), BF16),
        scratch_shapes=[pltpu.VMEM((VD, 2 * tq), F32)],
        compiler_params=_params("parallel", "parallel", "arbitrary"),
        name="attn_prompt",
    )(lam_params, q, k, v, bias_t, subln_g.reshape(VD, 1))


def _attn_sample_kernel(pt_ref, lam_ref, q_ref, kn_ref, vn_ref, kc_ref, vc_ref, bias_ref, biasn_ref, g_ref,
                        o_ref, m_scr, l_scr, acc_scr, *, n_pages, dec_seq, lam_init):
    del pt_ref
    j = pl.program_id(1)

    @pl.when(j == 0)
    def _():
        m_scr[...] = jnp.full_like(m_scr, NEG_INF)
        l_scr[...] = jnp.zeros_like(l_scr)
        acc_scr[...] = jnp.zeros_like(acc_scr)

    def update(h, qq, s, vb):
        m_old = m_scr[h]
        m_new = jnp.maximum(m_old, jnp.max(s, axis=-1, keepdims=True))
        alpha = jnp.exp(m_old - m_new)
        p = jnp.exp(s - m_new)
        l_scr[h] = alpha * l_scr[h] + jnp.sum(p, axis=-1, keepdims=True)
        acc_scr[h] = alpha * acc_scr[h] + jnp.dot(p.astype(BF16), vb, preferred_element_type=F32)
        m_scr[h] = m_new

    for h in range(N_HEADS):
        qq = _split_maps(q_ref[:, h * VD:(h + 1) * VD]).astype(BF16)
        kb = kc_ref[pl.ds(h, PAGE_ROWS, stride=N_HEADS), :].astype(BF16)
        vb = vc_ref[pl.ds(h, PAGE_ROWS, stride=N_HEADS), :].astype(BF16)
        s = lax.dot_general(qq, kb, (((1,), (1,)), ((), ())), preferred_element_type=F32)
        update(h, qq, s + bias_ref[0, h], vb)

    @pl.when(j == n_pages - 1)
    def _():
        lam = _lam_value(lam_ref, lam_init)
        pad = jnp.zeros((PAGE_ROWS - dec_seq, VD), F32)
        for h in range(N_HEADS):
            qq = _split_maps(q_ref[:, h * VD:(h + 1) * VD]).astype(BF16)
            kb = jnp.concatenate([kn_ref[:, h * VD:(h + 1) * VD], pad], axis=0).astype(BF16)
            vb = jnp.concatenate([vn_ref[:, h * VD:(h + 1) * VD], pad], axis=0).astype(BF16)
            s = lax.dot_general(qq, kb, (((1,), (1,)), ((), ())), preferred_element_type=F32)
            update(h, qq, s + biasn_ref[h], vb)
            acc = acc_scr[h] / l_scr[h]
            o = acc[:dec_seq] - lam * acc[dec_seq:]
            o = o * lax.rsqrt(jnp.mean(o * o, axis=-1, keepdims=True) + RMS_EPS)
            o_ref[:, h * VD:(h + 1) * VD] = o * g_ref[...] * (1.0 - lam_init)


PAGE_ROWS = 128


def _attn_sample(layer, page_table, q, k_new, v_new, cache_k, cache_v, bias_pages, bias_new,
                 lam_params, subln_g, lam_init, dec_batch, dec_seq):
    n_pages = page_table.shape[1]
    rows = PAGE_ROWS * N_HEADS
    tok_spec = pl.BlockSpec((dec_seq, ATTN_WIDTH), lambda b, j, pt: (b, 0))
    page_spec = pl.BlockSpec((None, None, rows, VD), lambda b, j, pt: (layer, pt[b, j], 0, 0))
    grid_spec = pltpu.PrefetchScalarGridSpec(
        num_scalar_prefetch=1,
        grid=(dec_batch, n_pages),
        in_specs=[pl.BlockSpec((4, DK), lambda b, j, pt: (0, 0)),
                  tok_spec, tok_spec, tok_spec, page_spec, page_spec,
                  pl.BlockSpec((1, N_HEADS, 2 * dec_seq, PAGE_ROWS), lambda b, j, pt: (j, 0, 0, 0)),
                  pl.BlockSpec((N_HEADS, 2 * dec_seq, PAGE_ROWS), lambda b, j, pt: (0, 0, 0)),
                  pl.BlockSpec((1, VD), lambda b, j, pt: (0, 0))],
        out_specs=tok_spec,
        scratch_shapes=[pltpu.VMEM((N_HEADS, 2 * dec_seq, 1), F32),
                        pltpu.VMEM((N_HEADS, 2 * dec_seq, 1), F32),
                        pltpu.VMEM((N_HEADS, 2 * dec_seq, VD), F32)])
    return pl.pallas_call(
        functools.partial(_attn_sample_kernel, n_pages=n_pages, dec_seq=dec_seq, lam_init=lam_init),
        grid_spec=grid_spec,
        out_shape=jax.ShapeDtypeStruct((dec_batch * dec_seq, ATTN_WIDTH), F32),
        compiler_params=_params("parallel", "arbitrary"),
        name="attn_sample",
    )(page_table, lam_params, q, k_new, v_new, cache_k, cache_v, bias_pages, bias_new,
      subln_g.reshape(1, VD))


def _ssm_weights(a_re, a_im, b_re, b_im, c_re, c_im, d, log_dt):
    hp = lax.Precision.HIGHEST
    g, n, p, c = SSM_GROUPS, SSM_STATE, SSM_GROUP, SSM_CHUNK
    dt = jnp.exp(log_dt.astype(F32))[:, None]
    a_re, a_im = a_re.astype(F32), a_im.astype(F32)
    tau = jnp.arange(c + 1, dtype=F32)[:, None, None]
    mag = jnp.exp(a_re * dt * tau)
    ang = a_im * dt * tau
    pw_re, pw_im = mag * jnp.cos(ang), mag * jnp.sin(ang)
    nr, ni = pw_re[1] - 1.0, pw_im[1]
    den = a_re * a_re + a_im * a_im
    f_re, f_im = (nr * a_re + ni * a_im) / den, (ni * a_re - nr * a_im) / den
    b_re, b_im = b_re.astype(F32), b_im.astype(F32)
    bb_re = f_re[:, :, None] * b_re - f_im[:, :, None] * b_im
    bb_im = f_re[:, :, None] * b_im + f_im[:, :, None] * b_re
    c_re, c_im = c_re.astype(F32), c_im.astype(F32)
    cp_re = c_re[None] * pw_re[:, :, None, :] - c_im[None] * pw_im[:, :, None, :]
    cp_im = c_re[None] * pw_im[:, :, None, :] + c_im[None] * pw_re[:, :, None, :]
    kern = (jnp.einsum('tgpn,gnq->tgpq', cp_re, bb_re, precision=hp)
            - jnp.einsum('tgpn,gnq->tgpq', cp_im, bb_im, precision=hp))
    ss, tt = jnp.meshgrid(jnp.arange(c), jnp.arange(c), indexing='ij')
    lag = tt - ss
    m_intra = jnp.where((lag >= 0)[:, :, None, None, None], kern[jnp.maximum(lag, 0)], 0.0)
    m_intra = jnp.transpose(m_intra, (2, 0, 4, 1, 3)).reshape(g, c * p, c * p)
    rev = pw_re[c - 1::-1][:c], pw_im[c - 1::-1][:c]
    wb_re = rev[0][:, :, :, None] * bb_re[None] - rev[1][:, :, :, None] * bb_im[None]
    wb_im = rev[0][:, :, :, None] * bb_im[None] + rev[1][:, :, :, None] * bb_re[None]
    wb_re = jnp.transpose(wb_re, (1, 0, 3, 2)).reshape(g, c * p, n)
    wb_im = jnp.transpose(wb_im, (1, 0, 3, 2)).reshape(g, c * p, n)
    wc_re = jnp.transpose(cp_re[1:], (1, 3, 0, 2)).reshape(g, n, c * p)
    wc_im = -jnp.transpose(cp_im[1:], (1, 3, 0, 2)).reshape(g, n, c * p)

    def pair_diag(x):
        r, cc = x.shape[1:]
        x = x.reshape(g // 2, 2, r, cc)
        return jnp.einsum('kirc,ij->kirjc', x, jnp.eye(2, dtype=F32)).reshape(g // 2, 2 * r, 2 * cc)

    d_chunk = jnp.broadcast_to(d.astype(F32).reshape(g, 1, p), (g, c, p)).reshape(1, g * c * p)
    return dict(m=pair_diag(m_intra), wb_re=pair_diag(wb_re), wb_im=pair_diag(wb_im),
                wc_re=pair_diag(wc_re), wc_im=pair_diag(wc_im),
                a_re=pw_re[c].reshape(1, g * n), a_im=pw_im[c].reshape(1, g * n), d=d_chunk)


def _ssm_kernel(u_ref, h0re_ref, h0im_ref, m_ref, wbre_ref, wbim_ref, wcre_ref, wcim_ref,
                are_ref, aim_ref, d_ref, z_ref, hre_ref, him_ref,
                sre_scr, sim_scr, hsre_scr, hsim_scr, *, lc, bt, cdt, precision):
    ci = pl.program_id(2)
    rows = lc * bt
    sw = 2 * SSM_STATE
    dot = functools.partial(jnp.dot, preferred_element_type=F32, precision=precision)

    @pl.when(ci == 0)
    def _():
        hre_ref[...] = h0re_ref[...]
        him_ref[...] = h0im_ref[...]

    for k in range(SSM_PAIRS_PER_STEP):
        ug = u_ref[:, :, k * SSM_PAIR:(k + 1) * SSM_PAIR].reshape(rows, SSM_PAIR).astype(cdt)
        sre_scr[:, k * sw:(k + 1) * sw] = dot(ug, wbre_ref[k])
        sim_scr[:, k * sw:(k + 1) * sw] = dot(ug, wbim_ref[k])

    a_re = are_ref[...]
    a_im = aim_ref[...]

    def step(c, carry):
        h_re, h_im = carry
        r0 = pl.multiple_of(c * bt, bt)
        hsre_scr[pl.ds(r0, bt), :] = h_re
        hsim_scr[pl.ds(r0, bt), :] = h_im
        n_re = a_re * h_re - a_im * h_im + sre_scr[pl.ds(r0, bt), :]
        n_im = a_re * h_im + a_im * h_re + sim_scr[pl.ds(r0, bt), :]
        return n_re, n_im

    h_re, h_im = lax.fori_loop(0, lc, step, (hre_ref[...], him_ref[...]))
    hre_ref[...] = h_re
    him_ref[...] = h_im

    for k in range(SSM_PAIRS_PER_STEP):
        lanes = slice(k * SSM_PAIR, (k + 1) * SSM_PAIR)
        ug = u_ref[:, :, lanes].reshape(rows, SSM_PAIR)
        y = (dot(ug.astype(cdt), m_ref[k])
             + dot(hsre_scr[:, k * sw:(k + 1) * sw].astype(cdt), wcre_ref[k])
             + dot(hsim_scr[:, k * sw:(k + 1) * sw].astype(cdt), wcim_ref[k])
             + d_ref[:, lanes] * ug)
        z_ref[:, :, lanes] = _gelu(y).reshape(lc, bt, SSM_PAIR)


def _ssm(u_chunks, h0_re, h0_im, w, lc, bt, cdt, precision):
    nc, bsz, width = u_chunks.shape
    pps = SSM_PAIRS_PER_STEP
    sw = 2 * SSM_STATE
    u_spec = pl.BlockSpec((lc, bt, pps * SSM_PAIR), lambda p, b, c: (c, b, p))
    st_spec = pl.BlockSpec((bt, pps * sw), lambda p, b, c: (b, p))

    def wspec(r, cc):
        return pl.BlockSpec((pps, r, cc), lambda p, b, c: (p, 0, 0))

    return pl.pallas_call(
        functools.partial(_ssm_kernel, lc=lc, bt=bt, cdt=cdt, precision=precision),
        grid=(SSM_PAIRS // pps, bsz // bt, nc // lc),
        in_specs=[u_spec, st_spec, st_spec,
                  wspec(SSM_PAIR, SSM_PAIR), wspec(SSM_PAIR, sw), wspec(SSM_PAIR, sw),
                  wspec(sw, SSM_PAIR), wspec(sw, SSM_PAIR),
                  pl.BlockSpec((1, pps * sw), lambda p, b, c: (0, p)),
                  pl.BlockSpec((1, pps * sw), lambda p, b, c: (0, p)),
                  pl.BlockSpec((1, pps * SSM_PAIR), lambda p, b, c: (0, p))],
        out_specs=[u_spec, st_spec, st_spec],
        out_shape=[jax.ShapeDtypeStruct((nc, bsz, width), F32),
                   jax.ShapeDtypeStruct(h0_re.shape, F32),
                   jax.ShapeDtypeStruct(h0_im.shape, F32)],
        scratch_shapes=[pltpu.VMEM((lc * bt, pps * sw), F32) for _ in range(4)],
        compiler_params=_params("parallel", "parallel", "arbitrary"),
        name="ssm_scan",
    )(u_chunks, h0_re, h0_im, w['m'].astype(cdt), w['wb_re'].astype(cdt), w['wb_im'].astype(cdt),
      w['wc_re'].astype(cdt), w['wc_im'].astype(cdt), w['a_re'], w['a_im'], w['d'])


def _to_chunks(u, bsz, t):
    nc = t // SSM_CHUNK
    u = u.reshape(bsz, nc, SSM_CHUNK, SSM_GROUPS, SSM_GROUP)
    return jnp.transpose(u, (1, 0, 3, 2, 4)).reshape(nc, bsz, SSM_GROUPS * SSM_CHUNK * SSM_GROUP)


def _from_chunks(z, bsz, t):
    nc = t // SSM_CHUNK
    z = z.reshape(nc, bsz, SSM_GROUPS, SSM_CHUNK, SSM_GROUP)
    return jnp.transpose(z, (1, 0, 3, 2, 4)).reshape(bsz * t, SSM_WIDTH)


def _glu_kernel(z_ref, w_ref, b_ref, o_ref):
    z = z_ref[...]
    gate = jnp.dot(z.astype(BF16), w_ref[...], preferred_element_type=F32) + b_ref[...]
    o_ref[...] = (z * _sigmoid(gate)).astype(o_ref.dtype)


def _glu(z, w_bf16, b, tm=512):
    m = z.shape[0]
    return pl.pallas_call(
        _glu_kernel,
        grid=(m // tm,),
        in_specs=[pl.BlockSpec((tm, SSM_WIDTH), lambda i: (i, 0)),
                  pl.BlockSpec((SSM_WIDTH, SSM_WIDTH), lambda i: (0, 0)),
                  pl.BlockSpec((1, SSM_WIDTH), lambda i: (0, 0))],
        out_specs=pl.BlockSpec((tm, SSM_WIDTH), lambda i: (i, 0)),
        out_shape=jax.ShapeDtypeStruct((m, SSM_WIDTH), BF16),
        compiler_params=_params("parallel"),
        name="ssm_glu",
    )(z, w_bf16, b.reshape(1, SSM_WIDTH))


def _out_proj_kernel(attn_ref, ssm_ref, x_ref, w_ref, g_ref, h_ref, xn_ref):
    mix = (jnp.dot(attn_ref[...].astype(BF16), w_ref[:ATTN_WIDTH, :], preferred_element_type=F32)
           + jnp.dot(ssm_ref[...], w_ref[ATTN_WIDTH:, :], preferred_element_type=F32))
    h = x_ref[...] + mix
    h_ref[...] = h
    xn_ref[...] = _rms(h, g_ref[...]).astype(BF16)


def _out_proj(attn, ssm_out, x, w_bf16, g, tm=256):
    m = x.shape[0]
    return pl.pallas_call(
        _out_proj_kernel,
        grid=(m // tm,),
        in_specs=[pl.BlockSpec((tm, ATTN_WIDTH), lambda i: (i, 0)),
                  pl.BlockSpec((tm, SSM_WIDTH), lambda i: (i, 0)),
                  pl.BlockSpec((tm, D_MODEL), lambda i: (i, 0)),
                  pl.BlockSpec((D_MODEL, D_MODEL), lambda i: (0, 0)),
                  pl.BlockSpec((1, D_MODEL), lambda i: (0, 0))],
        out_specs=[pl.BlockSpec((tm, D_MODEL), lambda i: (i, 0)),
                   pl.BlockSpec((tm, D_MODEL), lambda i: (i, 0))],
        out_shape=[jax.ShapeDtypeStruct((m, D_MODEL), F32),
                   jax.ShapeDtypeStruct((m, D_MODEL), BF16)],
        compiler_params=_params("parallel"),
        name="out_proj",
    )(attn, ssm_out, x, w_bf16, g.reshape(1, D_MODEL))


def _top16(s):
    iota = lax.broadcasted_iota(jnp.int32, s.shape, 0)
    rank = jnp.full(s.shape, 31, jnp.int32)
    vals = []
    for r in range(PEER_TOPK):
        mx = jnp.max(s, axis=0, keepdims=True)
        idx = jnp.min(jnp.where(s == mx, iota, s.shape[0]), axis=0, keepdims=True)
        hit = iota == idx
        rank = jnp.where(hit, r, rank)
        s = jnp.where(hit, -jnp.inf, s)
        vals.append(mx)
    return vals, rank


def _peer_route_kernel(xn_ref, wq_ref, keys_ref, lena_ref, rkb_ref, ea_ref, eb_ref,
                       q_scr, cand_scr, sel_scr):
    tm = xn_ref.shape[0]
    q_scr[...] = jnp.dot(xn_ref[...], wq_ref[...], preferred_element_type=F32).astype(BF16)
    cand_scr[...] = jnp.full_like(cand_scr, -jnp.inf)
    nt = (((1,), (1,)), ((), ()))

    def head(idx, _):
        h = idx // (tm // LANES)
        sub = idx % (tm // LANES)
        r0 = pl.multiple_of(sub * LANES, LANES)
        c0 = pl.multiple_of(h * 2 * PEER_HALF, 2 * PEER_HALF)
        qa = q_scr[pl.ds(r0, LANES), pl.ds(c0, PEER_HALF)]
        qb = q_scr[pl.ds(r0, LANES), pl.ds(c0 + PEER_HALF, PEER_HALF)]
        s_a = lax.dot_general(keys_ref[0], qa, nt, preferred_element_type=F32)
        s_b = lax.dot_general(keys_ref[1], qb, nt, preferred_element_type=F32)
        va, ra = _top16(s_a)
        vb, rb = _top16(s_b)
        for p, (i, j) in enumerate(_PEER_CAND):
            cand_scr[p:p + 1, :] = va[i] + vb[j]
        cand = cand_scr[...]
        iota = lax.broadcasted_iota(jnp.int32, cand.shape, 0)
        work = cand
        sel = jnp.zeros(cand.shape, F32)
        for _r in range(PEER_TOPK):
            mx = jnp.max(work, axis=0, keepdims=True)
            pos = jnp.min(jnp.where(work == mx, iota, cand.shape[0]), axis=0, keepdims=True)
            hit = iota == pos
            sel = jnp.where(hit, 1.0, sel)
            work = jnp.where(hit, -jnp.inf, work)
        top = va[0] + vb[0]
        z = jnp.sum(jnp.where(sel > 0, jnp.exp(cand - top), 0.0), axis=0, keepdims=True)
        sel_scr[...] = sel
        lena = jnp.zeros(s_a.shape, F32)
        p = 0
        for i in range(PEER_TOPK):
            n_i = PEER_TOPK // (i + 1)
            len_i = jnp.sum(sel_scr[p:p + n_i, :], axis=0, keepdims=True)
            lena = jnp.where(ra == i, len_i, lena)
            p += n_i
        lena_ref[h, :, pl.ds(r0, LANES)] = lena
        rkb_ref[h, :, pl.ds(r0, LANES)] = rb.astype(F32)
        ea_ref[h, :, pl.ds(r0, LANES)] = jnp.exp(s_a - va[0]) * (1.0 / z)
        eb_ref[h, :, pl.ds(r0, LANES)] = jnp.exp(s_b - vb[0])
        return 0

    lax.fori_loop(0, PEER_HEADS * (tm // LANES), head, 0)


def _peer_route(xn, wq_bf16, keys_bf16, tm=512):
    m = xn.shape[0]
    out_spec = pl.BlockSpec((PEER_HEADS, PEER_NKEYS, tm), lambda i: (0, 0, i))
    out_sds = jax.ShapeDtypeStruct((PEER_HEADS, PEER_NKEYS, m), F32)
    return pl.pallas_call(
        _peer_route_kernel,
        grid=(m // tm,),
        in_specs=[pl.BlockSpec((tm, D_MODEL), lambda i: (i, 0)),
                  pl.BlockSpec((D_MODEL, D_MODEL), lambda i: (0, 0)),
                  pl.BlockSpec((2, PEER_NKEYS, PEER_HALF), lambda i: (0, 0, 0))],
        out_specs=[out_spec] * 4,
        out_shape=[out_sds] * 4,
        scratch_shapes=[pltpu.VMEM((tm, D_MODEL), BF16),
                        pltpu.VMEM((_PEER_CAND_ROWS, LANES), F32),
                        pltpu.VMEM((_PEER_CAND_ROWS, LANES), F32)],
        compiler_params=_params("parallel"),
        name="peer_route",
    )(xn, wq_bf16, keys_bf16)


def _peer_dense_kernel(xn_ref, u_ref, v_ref, lena_ref, rkb_ref, ea_ref, eb_ref, o_ref, *, ta):
    j = pl.program_id(1)
    pre = lax.dot_general(u_ref[...], xn_ref[...], (((1,), (1,)), ((), ())),
                          preferred_element_type=F32)
    act = _gelu(pre)
    parts = []
    for t in range(ta):
        a = j * ta + t
        w = None
        for h in range(PEER_HEADS):
            lena = lena_ref[h, pl.ds(a, 1), :]
            ea = ea_ref[h, pl.ds(a, 1), :]
            term = jnp.where(rkb_ref[h] < lena, eb_ref[h], 0.0) * ea
            w = term if w is None else w + term
        parts.append((w * act[t * PEER_NKEYS:(t + 1) * PEER_NKEYS]).astype(BF16))
    ht = jnp.concatenate(parts, axis=0)
    contrib = lax.dot_general(ht, v_ref[...], (((0,), (0,)), ((), ())), preferred_element_type=F32)

    @pl.when(j == 0)
    def _():
        o_ref[...] = contrib

    @pl.when(j > 0)
    def _():
        o_ref[...] += contrib


def _peer_dense(xn, u_bf16, v_bf16, route, tm=512, ta=4):
    m = xn.shape[0]
    r_spec = pl.BlockSpec((PEER_HEADS, PEER_NKEYS, tm), lambda i, j: (0, 0, i))
    e_spec = pl.BlockSpec((ta * PEER_NKEYS, D_MODEL), lambda i, j: (j, 0))
    return pl.pallas_call(
        functools.partial(_peer_dense_kernel, ta=ta),
        grid=(m // tm, PEER_NKEYS // ta),
        in_specs=[pl.BlockSpec((tm, D_MODEL), lambda i, j: (i, 0)), e_spec, e_spec,
                  r_spec, r_spec, r_spec, r_spec],
        out_specs=pl.BlockSpec((tm, D_MODEL), lambda i, j: (i, 0)),
        out_shape=jax.ShapeDtypeStruct((m, D_MODEL), F32),
        compiler_params=_params("parallel", "arbitrary"),
        name="peer_dense",
    )(xn, u_bf16, v_bf16, *route)


def _ple_kernel(h_ref, peer_ref, ple_ref, pw_ref, gw_ref, g_ref, o_ref, *, final_norm):
    h = h_ref[...] + peer_ref[...]
    gate = _sigmoid(jnp.dot(h.astype(BF16), gw_ref[...], preferred_element_type=F32))
    emb = jnp.dot(ple_ref[...].astype(BF16), pw_ref[...], preferred_element_type=F32)
    h = h + emb * gate
    if final_norm:
        h = _rms(h, g_ref[...])
    o_ref[...] = h


def _ple(h, peer_out, ple, pw_bf16, gw_bf16, g, final_norm, tm=256):
    m = h.shape[0]
    row = pl.BlockSpec((tm, D_MODEL), lambda i: (i, 0))
    return pl.pallas_call(
        functools.partial(_ple_kernel, final_norm=final_norm),
        grid=(m // tm,),
        in_specs=[row, row,
                  pl.BlockSpec((tm, PLE_DIM), lambda i: (i, 0)),
                  pl.BlockSpec((PLE_DIM, D_MODEL), lambda i: (0, 0)),
                  pl.BlockSpec((D_MODEL, D_MODEL), lambda i: (0, 0)),
                  pl.BlockSpec((1, D_MODEL), lambda i: (0, 0))],
        out_specs=row,
        out_shape=jax.ShapeDtypeStruct((m, D_MODEL), F32),
        compiler_params=_params("parallel"),
        name="ple_gate",
    )(h, peer_out, ple, pw_bf16, gw_bf16, g.reshape(1, D_MODEL))


def kernel(x_prompt, x_sample, cache_k, cache_v, state_ssm_re, state_ssm_im, page_table, p_prompt, p_sample, norm_mix_g, norm_ffn_g, w_in, w_out, lam_q1, lam_k1, lam_q2, lam_k2, subln_g, rel_bias, ssm_a_re, ssm_a_im, ssm_b_re, ssm_b_im, ssm_c_re, ssm_c_im, ssm_d, ssm_log_dt, ssm_w_glu, ssm_b_glu, peer_w_q, peer_keys, peer_u, peer_v, ple_w, ple_gate_w, final_norm_g):
    bsz, seq = x_prompt.shape[:2]
    dec_batch, dec_seq = x_sample.shape[:2]
    n_pool = cache_k.shape[1]
    n_pages = page_table.shape[1]
    past_len = n_pages * PAGE_ROWS
    tq = 256
    nq = seq // tq

    d_i, c_i, r_i = jnp.meshgrid(jnp.arange(nq), jnp.arange(tq), jnp.arange(tq), indexing='ij')
    bias_prompt = _rel_bias_by_distance(rel_bias, d_i * tq + r_i - c_i)
    j_i, t_i, c_i = jnp.meshgrid(jnp.arange(n_pages), jnp.arange(dec_seq), jnp.arange(PAGE_ROWS), indexing='ij')
    bias_pages = _rel_bias_by_distance(rel_bias, past_len + t_i - j_i * PAGE_ROWS - c_i)
    bias_pages = jnp.transpose(jnp.concatenate([bias_pages, bias_pages], axis=2), (1, 0, 2, 3))
    t_i, s_i = jnp.meshgrid(jnp.arange(dec_seq), jnp.arange(dec_seq), indexing='ij')
    bias_new = _rel_bias_by_distance(rel_bias, t_i - s_i)
    bias_new = jnp.pad(jnp.concatenate([bias_new, bias_new], axis=1),
                       ((0, 0), (0, 0), (0, PAGE_ROWS - dec_seq)), constant_values=NEG_INF)

    cache_k = cache_k.reshape(DEPTH, n_pool, PAGE_ROWS * N_HEADS, VD)
    cache_v = cache_v.reshape(DEPTH, n_pool, PAGE_ROWS * N_HEADS, VD)
    zeros_state = jnp.zeros((bsz, SSM_GROUPS * SSM_STATE), F32)

    hp = x_prompt.reshape(bsz * seq, D_MODEL)
    hs = x_sample.reshape(dec_batch * dec_seq, D_MODEL)
    outs = {k: [] for k in ('kp', 'vp', 'spr', 'spi', 'ks', 'vs', 'ssr', 'ssi')}

    for i in range(DEPTH):
        lam_init = 0.8 - 0.6 * math.exp(-0.3 * i)
        lam_params = jnp.stack([lam_q1[i], lam_k1[i], lam_q2[i], lam_k2[i]]).astype(F32)
        w_in_b = w_in[i].astype(BF16)
        w_out_b = w_out[i].astype(BF16)
        w_glu_b = ssm_w_glu[i].astype(BF16)
        wq_b = peer_w_q[i].astype(BF16)
        keys_b = peer_keys[i].astype(BF16)
        pu_b = peer_u[i].astype(BF16)
        pv_b = peer_v[i].astype(BF16)
        plw_b = ple_w[i].astype(BF16)
        gw_b = ple_gate_w[i].astype(BF16)
        ssm_w = _ssm_weights(ssm_a_re[i], ssm_a_im[i], ssm_b_re[i], ssm_b_im[i], ssm_c_re[i], ssm_c_im[i],
                             ssm_d[i], ssm_log_dt[i])
        last = i == DEPTH - 1

        def tail(h_in, attn, ssm_z, ple):
            ssm_out = _glu(ssm_z, w_glu_b, ssm_b_glu[i])
            h_mid, xn = _out_proj(attn, ssm_out, h_in, w_out_b, norm_ffn_g[i])
            route = _peer_route(xn, wq_b, keys_b)
            peer_out = _peer_dense(xn, pu_b, pv_b, route)
            return _ple(h_mid, peer_out, ple, plw_b, gw_b, final_norm_g, last)

        q, k_new, v_new, u = _in_proj(hp, norm_mix_g[i], w_in_b, BF16)
        attn = _attn_prompt(q, k_new, v_new, bias_prompt, lam_params, subln_g[i], lam_init, bsz, seq, tq)
        z, s_re, s_im = _ssm(_to_chunks(u, bsz, seq), zeros_state, zeros_state, ssm_w,
                             lc=min(128, seq // SSM_CHUNK), bt=8, cdt=BF16, precision=None)
        hp = tail(hp, attn, _from_chunks(z, bsz, seq), p_prompt[i].reshape(bsz * seq, PLE_DIM))
        outs['kp'].append(k_new.reshape(bsz, seq, N_HEADS, VD))
        outs['vp'].append(v_new.reshape(bsz, seq, N_HEADS, VD))
        outs['spr'].append(s_re.reshape(bsz, SSM_GROUPS, SSM_STATE))
        outs['spi'].append(s_im.reshape(bsz, SSM_GROUPS, SSM_STATE))

        q, k_new, v_new, u = _in_proj(hs, norm_mix_g[i], w_in_b, F32)
        attn = _attn_sample(i, page_table, q, k_new, v_new, cache_k, cache_v, bias_pages, bias_new,
                            lam_params, subln_g[i], lam_init, dec_batch, dec_seq)
        z, s_re, s_im = _ssm(_to_chunks(u, dec_batch, dec_seq),
                             state_ssm_re[i].astype(F32).reshape(dec_batch, -1),
                             state_ssm_im[i].astype(F32).reshape(dec_batch, -1), ssm_w,
                             lc=1, bt=dec_batch, cdt=F32, precision=lax.Precision.HIGHEST)
        hs = tail(hs, attn, _from_chunks(z, dec_batch, dec_seq), p_sample[i].reshape(dec_batch * dec_seq, PLE_DIM))
        outs['ks'].append(k_new.reshape(dec_batch, dec_seq, N_HEADS, VD))
        outs['vs'].append(v_new.reshape(dec_batch, dec_seq, N_HEADS, VD))
        outs['ssr'].append(s_re.reshape(dec_batch, SSM_GROUPS, SSM_STATE))
        outs['ssi'].append(s_im.reshape(dec_batch, SSM_GROUPS, SSM_STATE))

    return (hp.reshape(bsz, seq, D_MODEL), hs.reshape(dec_batch, dec_seq, D_MODEL),
            jnp.stack(outs['kp']), jnp.stack(outs['vp']), jnp.stack(outs['spr']), jnp.stack(outs['spi']),
            jnp.stack(outs['ks']), jnp.stack(outs['vs']), jnp.stack(outs['ssr']), jnp.stack(outs['ssi']))
```

```python
import functools
import math

import jax
import jax.numpy as jnp
from jax import lax
from jax.experimental import pallas as pl
from jax.experimental.pallas import tpu as pltpu

F32 = jnp.float32
BF16 = jnp.bfloat16

D_MODEL = 2048
DEPTH = 2
N_HEADS = 8
DK = 64
VD = 128
ATTN_WIDTH = N_HEADS * VD
SSM_WIDTH = 1024
SSM_GROUP = 16
SSM_GROUPS = 64
SSM_STATE = 64
N_BUCKETS = 32
MAX_DISTANCE = 128
PEER_HEADS = 8
PEER_NKEYS = 128
PEER_TOPK = 16
PEER_HALF = 128
PLE_DIM = 256
RMS_EPS = 1e-6
NEG_INF = -1e30

LANES = 128
SSM_CHUNK = 8
SSM_PAIR = 2 * SSM_CHUNK * SSM_GROUP
SSM_PAIRS = SSM_GROUPS // 2
SSM_PAIRS_PER_STEP = 4
VMEM_LIMIT = 50 * 1024 * 1024

_PEER_CAND = [(i, j) for i in range(PEER_TOPK) for j in range(PEER_TOPK) if (i + 1) * (j + 1) <= PEER_TOPK]
_PEER_CAND_ROWS = 56


def _params(*sem):
    return pltpu.CompilerParams(dimension_semantics=sem, vmem_limit_bytes=VMEM_LIMIT)


def _gelu(x):
    return 0.5 * x * (1.0 + jnp.tanh(math.sqrt(2.0 / math.pi) * (x + 0.044715 * (x * x * x))))


def _sigmoid(x):
    return 1.0 / (1.0 + jnp.exp(-x))


def _rms(x, g):
    return x * lax.rsqrt(jnp.mean(x * x, axis=-1, keepdims=True) + RMS_EPS) * g


def _in_proj_kernel(x_ref, g_ref, w_ref, q_ref, k_ref, v_ref, u_ref, xn_ref):
    j = pl.program_id(1)

    @pl.when(j == 0)
    def _():
        xn_ref[...] = _rms(x_ref[...], g_ref[...]).astype(BF16)

    acc = jnp.dot(xn_ref[...], w_ref[...], preferred_element_type=F32)

    @pl.when(j == 0)
    def _():
        q_ref[...] = (acc * DK ** -0.5).astype(q_ref.dtype)

    @pl.when(j == 1)
    def _():
        k_ref[...] = acc

    @pl.when(j == 2)
    def _():
        v_ref[...] = acc

    @pl.when(j == 3)
    def _():
        u_ref[...] = acc


def _in_proj(x, g, w_bf16, q_dtype, tm=512):
    m = x.shape[0]
    slab = pl.BlockSpec((tm, ATTN_WIDTH), lambda i, j: (i, 0))
    return pl.pallas_call(
        _in_proj_kernel,
        grid=(m // tm, 4),
        in_specs=[pl.BlockSpec((tm, D_MODEL), lambda i, j: (i, 0)),
                  pl.BlockSpec((1, D_MODEL), lambda i, j: (0, 0)),
                  pl.BlockSpec((D_MODEL, ATTN_WIDTH), lambda i, j: (0, j))],
        out_specs=[slab, slab, slab, slab],
        out_shape=[jax.ShapeDtypeStruct((m, ATTN_WIDTH), q_dtype),
                   jax.ShapeDtypeStruct((m, ATTN_WIDTH), F32),
                   jax.ShapeDtypeStruct((m, ATTN_WIDTH), F32),
                   jax.ShapeDtypeStruct((m, SSM_WIDTH), F32)],
        scratch_shapes=[pltpu.VMEM((tm, D_MODEL), BF16)],
        compiler_params=_params("parallel", "arbitrary"),
        name="in_proj",
    )(x, g.reshape(1, D_MODEL), w_bf16)


def _lam_value(lam_ref, lam_init):
    lp = lam_ref[...]
    l1 = jnp.sum(lp[0:1] * lp[1:2], axis=-1, keepdims=True)
    l2 = jnp.sum(lp[2:3] * lp[3:4], axis=-1, keepdims=True)
    return jnp.exp(l1) - jnp.exp(l2) + lam_init


def _split_maps(q):
    lane = lax.broadcasted_iota(jnp.int32, q.shape, 1)
    zero = jnp.zeros_like(q)
    return jnp.concatenate([jnp.where(lane < DK, q, zero), jnp.where(lane >= DK, q, zero)], axis=0)


def _bias_by_distance(rel_bias, lo, hi):
    rel = jnp.arange(lo, hi)
    n = jnp.maximum(rel, 0)
    max_exact = N_BUCKETS // 2
    nf = jnp.maximum(n, max_exact).astype(F32)
    large = max_exact + (jnp.log(nf / max_exact) / math.log(MAX_DISTANCE / max_exact)
                         * (N_BUCKETS - max_exact)).astype(jnp.int32)
    large = jnp.minimum(large, N_BUCKETS - 1)
    bucket = jnp.where(n < max_exact, n, large)
    onehot = bucket[:, None] == jnp.arange(N_BUCKETS)[None, :]
    bias = jnp.sum(jnp.where(onehot[None], rel_bias.astype(F32).T[:, None, :], 0.0), axis=-1)
    return jnp.where(rel >= 0, bias, NEG_INF)


def _prompt_bias_tiles(rel_bias, nq, tq):
    val = _bias_by_distance(rel_bias, -tq, nq * tq).reshape(N_HEADS, nq + 1, tq)
    v = jnp.concatenate([val[:, 1:], val[:, :-1]], axis=-1)
    x = jnp.broadcast_to(v[:, :, None, :], (N_HEADS, nq, tq, 2 * tq)).reshape(N_HEADS, nq, 2 * tq * tq)
    x = x[:, :, :tq * (2 * tq - 1)].reshape(N_HEADS, nq, tq, 2 * tq - 1)
    return x[:, :, :, :tq]


def _sample_bias_tables(rel_bias, n_pages, dec_seq):
    past_len = n_pages * PAGE_ROWS
    same_head = jnp.eye(N_HEADS, dtype=bool)
    val = _bias_by_distance(rel_bias, 0, past_len + dec_seq)
    bp = jnp.stack([val[:, t + 1:past_len + t + 1][:, ::-1] for t in range(dec_seq)], axis=1)
    bp = jnp.transpose(bp.reshape(N_HEADS, dec_seq, n_pages, PAGE_ROWS), (2, 0, 1, 3))
    bp = jnp.broadcast_to(bp[:, :, None], (n_pages, N_HEADS, 2, dec_seq, PAGE_ROWS))
    pages = jnp.where(same_head[None, :, None, None, None, :], bp[..., None], NEG_INF)
    pages = pages.reshape(n_pages, N_HEADS * 2 * dec_seq, PAGE_ROWS * N_HEADS)
    valn = _bias_by_distance(rel_bias, 1 - dec_seq, dec_seq)
    t_i, s_i = jnp.meshgrid(jnp.arange(dec_seq), jnp.arange(dec_seq), indexing='ij')
    bn = jnp.take(valn, t_i - s_i + dec_seq - 1, axis=1)
    bn = jnp.broadcast_to(bn[:, None], (N_HEADS, 2, dec_seq, dec_seq))
    new = jnp.where(same_head[:, None, None, None, :], bn[..., None], NEG_INF)
    new = new.reshape(N_HEADS * 2 * dec_seq, dec_seq * N_HEADS)
    new = jnp.pad(new, ((0, 0), (0, PAGE_ROWS - dec_seq * N_HEADS)), constant_values=NEG_INF)
    return pages, new


def _attn_prompt_kernel(lam_ref, q_ref, k_ref, v_ref, bias_ref, g_ref, o_ref, acc_scr, *, tq, lam_init):
    qi = pl.program_id(2)
    qq = _split_maps(q_ref[...])
    acc_scr[...] = jnp.zeros_like(acc_scr)

    def body(kj, carry):
        m_old, l_old = carry
        start = pl.multiple_of(kj * tq, tq)
        kb = k_ref[pl.ds(start, tq), :].astype(BF16)
        vb = v_ref[pl.ds(start, tq), :].astype(BF16)
        s = lax.dot_general(kb, qq, (((1,), (1,)), ((), ())), preferred_element_type=F32)
        bias = bias_ref[0, qi - kj]
        s = s + jnp.concatenate([bias, bias], axis=1)
        m_new = jnp.maximum(m_old, jnp.max(s, axis=0, keepdims=True))
        alpha = jnp.exp(m_old - m_new)
        p = jnp.exp(s - m_new)
        l_new = alpha * l_old + jnp.sum(p, axis=0, keepdims=True)
        pv = lax.dot_general(vb, p.astype(BF16), (((0,), (0,)), ((), ())), preferred_element_type=F32)
        acc_scr[...] = alpha * acc_scr[...] + pv
        return m_new, l_new

    m0 = jnp.full((1, 2 * tq), NEG_INF, F32)
    l0 = jnp.zeros((1, 2 * tq), F32)
    _, l_fin = lax.fori_loop(0, qi + 1, body, (m0, l0))

    lam = _lam_value(lam_ref, lam_init)
    acc = acc_scr[...] / l_fin
    o = acc[:, :tq] - lam * acc[:, tq:]
    o = o * lax.rsqrt(jnp.mean(o * o, axis=0, keepdims=True) + RMS_EPS)
    o = o * g_ref[...] * (1.0 - lam_init)
    o_ref[...] = o.T.astype(o_ref.dtype)


def _attn_prompt(q, k, v, bias_t, lam_params, subln_g, lam_init, bsz, seq, tq=256):
    nq = seq // tq
    kv_spec = pl.BlockSpec((seq, VD), lambda b, h, i: (b, h))
    return pl.pallas_call(
        functools.partial(_attn_prompt_kernel, tq=tq, lam_init=lam_init),
        grid=(bsz, N_HEADS, nq),
        in_specs=[pl.BlockSpec((4, DK), lambda b, h, i: (0, 0)),
                  pl.BlockSpec((tq, VD), lambda b, h, i: (b * nq + i, h)),
                  kv_spec, kv_spec,
                  pl.BlockSpec((1, nq, tq, tq), lambda b, h, i: (h, 0, 0, 0)),
                  pl.BlockSpec((VD, 1), lambda b, h, i: (0, 0))],
        out_specs=pl.BlockSpec((tq, VD), lambda b, h, i: (b * nq + i, h)),
        out_shape=jax.ShapeDtypeStruct((bsz * seq, ATTN_WIDTH), BF16),
        scratch_shapes=[pltpu.VMEM((VD, 2 * tq), F32)],
        compiler_params=_params("parallel", "parallel", "arbitrary"),
        name="attn_prompt",
    )(lam_params, q, k, v, bias_t, subln_g.reshape(VD, 1))


PAGE_ROWS = 128


def _attn_sample_kernel(pt_ref, lam_ref, q_ref, kn_ref, vn_ref, kc_ref, vc_ref, bias_ref, biasn_ref, g_ref,
                        o_ref, q_scr, m_scr, l_scr, acc_scr, *, n_pages, dec_seq, lam_init):
    del pt_ref
    j = pl.program_id(1)
    nt = (((1,), (1,)), ((), ()))

    @pl.when(j == 0)
    def _():
        parts = [_split_maps(q_ref[:, h * VD:(h + 1) * VD]) for h in range(N_HEADS)]
        q_scr[...] = jnp.concatenate(parts, axis=0).astype(BF16)
        m_scr[...] = jnp.full_like(m_scr, NEG_INF)
        l_scr[...] = jnp.zeros_like(l_scr)
        acc_scr[...] = jnp.zeros_like(acc_scr)

    def update(kb, vb, bias):
        s = lax.dot_general(q_scr[...], kb, nt, preferred_element_type=F32) + bias
        m_old = m_scr[...]
        m_new = jnp.maximum(m_old, jnp.max(s, axis=-1, keepdims=True))
        alpha = jnp.exp(m_old - m_new)
        p = jnp.exp(s - m_new)
        l_scr[...] = alpha * l_scr[...] + jnp.sum(p, axis=-1, keepdims=True)
        acc_scr[...] = alpha * acc_scr[...] + jnp.dot(p.astype(BF16), vb, preferred_element_type=F32)
        m_scr[...] = m_new

    update(kc_ref[...].astype(BF16), vc_ref[...].astype(BF16), bias_ref[j])

    @pl.when(j == n_pages - 1)
    def _():
        pad = jnp.zeros((PAGE_ROWS - dec_seq * N_HEADS, VD), F32)
        update(jnp.concatenate([kn_ref[...], pad], axis=0).astype(BF16),
               jnp.concatenate([vn_ref[...], pad], axis=0).astype(BF16), biasn_ref[...])
        lam = _lam_value(lam_ref, lam_init)
        acc = acc_scr[...] / l_scr[...]
        for h in range(N_HEADS):
            r0 = h * 2 * dec_seq
            o = acc[r0:r0 + dec_seq] - lam * acc[r0 + dec_seq:r0 + 2 * dec_seq]
            o = o * lax.rsqrt(jnp.mean(o * o, axis=-1, keepdims=True) + RMS_EPS)
            o_ref[:, h * VD:(h + 1) * VD] = o * g_ref[...] * (1.0 - lam_init)


def _attn_sample(layer, page_table, q, k_new, v_new, cache_k, cache_v, bias_pages, bias_new,
                 lam_params, subln_g, lam_init, dec_batch, dec_seq):
    n_pages = page_table.shape[1]
    rows = PAGE_ROWS * N_HEADS
    qrows = N_HEADS * 2 * dec_seq
    tok_spec = pl.BlockSpec((dec_seq, ATTN_WIDTH), lambda b, j, pt: (b, 0))
    new_spec = pl.BlockSpec((dec_seq * N_HEADS, VD), lambda b, j, pt: (b, 0))
    page_spec = pl.BlockSpec((None, None, rows, VD), lambda b, j, pt: (layer, pt[b, j], 0, 0))
    grid_spec = pltpu.PrefetchScalarGridSpec(
        num_scalar_prefetch=1,
        grid=(dec_batch, n_pages),
        in_specs=[pl.BlockSpec((4, DK), lambda b, j, pt: (0, 0)),
                  tok_spec, new_spec, new_spec, page_spec, page_spec,
                  pl.BlockSpec((n_pages, qrows, rows), lambda b, j, pt: (0, 0, 0)),
                  pl.BlockSpec((qrows, PAGE_ROWS), lambda b, j, pt: (0, 0)),
                  pl.BlockSpec((1, VD), lambda b, j, pt: (0, 0))],
        out_specs=tok_spec,
        scratch_shapes=[pltpu.VMEM((qrows, VD), BF16),
                        pltpu.VMEM((qrows, 1), F32),
                        pltpu.VMEM((qrows, 1), F32),
                        pltpu.VMEM((qrows, VD), F32)])
    return pl.pallas_call(
        functools.partial(_attn_sample_kernel, n_pages=n_pages, dec_seq=dec_seq, lam_init=lam_init),
        grid_spec=grid_spec,
        out_shape=jax.ShapeDtypeStruct((dec_batch * dec_seq, ATTN_WIDTH), F32),
        compiler_params=_params("parallel", "arbitrary"),
        name="attn_sample",
    )(page_table, lam_params, q, k_new.reshape(-1, VD), v_new.reshape(-1, VD), cache_k, cache_v,
      bias_pages, bias_new, subln_g.reshape(1, VD))


def _ssm_weights(a_re, a_im, b_re, b_im, c_re, c_im, d, log_dt):
    hp = lax.Precision.HIGHEST
    g, n, p, c = SSM_GROUPS, SSM_STATE, SSM_GROUP, SSM_CHUNK
    dt = jnp.exp(log_dt.astype(F32))[:, None]
    a_re, a_im = a_re.astype(F32), a_im.astype(F32)
    tau = jnp.arange(c + 1, dtype=F32)[:, None, None]
    mag = jnp.exp(a_re * dt * tau)
    ang = a_im * dt * tau
    pw_re, pw_im = mag * jnp.cos(ang), mag * jnp.sin(ang)
    nr, ni = pw_re[1] - 1.0, pw_im[1]
    den = a_re * a_re + a_im * a_im
    f_re, f_im = (nr * a_re + ni * a_im) / den, (ni * a_re - nr * a_im) / den
    b_re, b_im = b_re.astype(F32), b_im.astype(F32)
    bb_re = f_re[:, :, None] * b_re - f_im[:, :, None] * b_im
    bb_im = f_re[:, :, None] * b_im + f_im[:, :, None] * b_re
    c_re, c_im = c_re.astype(F32), c_im.astype(F32)
    cp_re = c_re[None] * pw_re[:, :, None, :] - c_im[None] * pw_im[:, :, None, :]
    cp_im = c_re[None] * pw_im[:, :, None, :] + c_im[None] * pw_re[:, :, None, :]
    kern = (jnp.einsum('tgpn,gnq->tgpq', cp_re, bb_re, precision=hp)
            - jnp.einsum('tgpn,gnq->tgpq', cp_im, bb_im, precision=hp))
    ss, tt = jnp.meshgrid(jnp.arange(c), jnp.arange(c), indexing='ij')
    lag = tt - ss
    m_intra = jnp.where((lag >= 0)[:, :, None, None, None], kern[jnp.maximum(lag, 0)], 0.0)
    m_intra = jnp.transpose(m_intra, (2, 0, 4, 1, 3)).reshape(g, c * p, c * p)
    rev = pw_re[c - 1::-1][:c], pw_im[c - 1::-1][:c]
    wb_re = rev[0][:, :, :, None] * bb_re[None] - rev[1][:, :, :, None] * bb_im[None]
    wb_im = rev[0][:, :, :, None] * bb_im[None] + rev[1][:, :, :, None] * bb_re[None]
    wb_re = jnp.transpose(wb_re, (1, 0, 3, 2)).reshape(g, c * p, n)
    wb_im = jnp.transpose(wb_im, (1, 0, 3, 2)).reshape(g, c * p, n)
    wc_re = jnp.transpose(cp_re[1:], (1, 3, 0, 2)).reshape(g, n, c * p)
    wc_im = -jnp.transpose(cp_im[1:], (1, 3, 0, 2)).reshape(g, n, c * p)

    def pair_diag(x):
        r, cc = x.shape[1:]
        x = x.reshape(g // 2, 2, r, cc)
        return jnp.einsum('kirc,ij->kirjc', x, jnp.eye(2, dtype=F32)).reshape(g // 2, 2 * r, 2 * cc)

    d_chunk = jnp.broadcast_to(d.astype(F32).reshape(g, 1, p), (g, c, p)).reshape(1, g * c * p)
    return dict(m=pair_diag(m_intra), wb_re=pair_diag(wb_re), wb_im=pair_diag(wb_im),
                wc_re=pair_diag(wc_re), wc_im=pair_diag(wc_im),
                a_re=pw_re[c].reshape(1, g * n), a_im=pw_im[c].reshape(1, g * n), d=d_chunk)


def _ssm_kernel(u_ref, h0re_ref, h0im_ref, m_ref, wbre_ref, wbim_ref, wcre_ref, wcim_ref,
                are_ref, aim_ref, d_ref, z_ref, hre_ref, him_ref,
                sre_scr, sim_scr, hsre_scr, hsim_scr, *, lc, bt, cdt, precision):
    ci = pl.program_id(2)
    rows = lc * bt
    sw = 2 * SSM_STATE
    dot = functools.partial(jnp.dot, preferred_element_type=F32, precision=precision)

    @pl.when(ci == 0)
    def _():
        hre_ref[...] = h0re_ref[...]
        him_ref[...] = h0im_ref[...]

    for k in range(SSM_PAIRS_PER_STEP):
        ug = u_ref[:, :, k * SSM_PAIR:(k + 1) * SSM_PAIR].reshape(rows, SSM_PAIR).astype(cdt)
        sre_scr[:, k * sw:(k + 1) * sw] = dot(ug, wbre_ref[k])
        sim_scr[:, k * sw:(k + 1) * sw] = dot(ug, wbim_ref[k])

    a_re = are_ref[...]
    a_im = aim_ref[...]

    def step(c, carry):
        h_re, h_im = carry
        r0 = pl.multiple_of(c * bt, bt)
        hsre_scr[pl.ds(r0, bt), :] = h_re
        hsim_scr[pl.ds(r0, bt), :] = h_im
        n_re = a_re * h_re - a_im * h_im + sre_scr[pl.ds(r0, bt), :]
        n_im = a_re * h_im + a_im * h_re + sim_scr[pl.ds(r0, bt), :]
        return n_re, n_im

    h_re, h_im = lax.fori_loop(0, lc, step, (hre_ref[...], him_ref[...]))
    hre_ref[...] = h_re
    him_ref[...] = h_im

    for k in range(SSM_PAIRS_PER_STEP):
        lanes = slice(k * SSM_PAIR, (k + 1) * SSM_PAIR)
        ug = u_ref[:, :, lanes].reshape(rows, SSM_PAIR)
        y = (dot(ug.astype(cdt), m_ref[k])
             + dot(hsre_scr[:, k * sw:(k + 1) * sw].astype(cdt), wcre_ref[k])
             + dot(hsim_scr[:, k * sw:(k + 1) * sw].astype(cdt), wcim_ref[k])
             + d_ref[:, lanes] * ug)
        z_ref[:, :, lanes] = _gelu(y).reshape(lc, bt, SSM_PAIR)


def _ssm(u_chunks, h0_re, h0_im, w, lc, bt, cdt, precision):
    nc, bsz, width = u_chunks.shape
    pps = SSM_PAIRS_PER_STEP
    sw = 2 * SSM_STATE
    u_spec = pl.BlockSpec((lc, bt, pps * SSM_PAIR), lambda p, b, c: (c, b, p))
    st_spec = pl.BlockSpec((bt, pps * sw), lambda p, b, c: (b, p))

    def wspec(r, cc):
        return pl.BlockSpec((pps, r, cc), lambda p, b, c: (p, 0, 0))

    return pl.pallas_call(
        functools.partial(_ssm_kernel, lc=lc, bt=bt, cdt=cdt, precision=precision),
        grid=(SSM_PAIRS // pps, bsz // bt, nc // lc),
        in_specs=[u_spec, st_spec, st_spec,
                  wspec(SSM_PAIR, SSM_PAIR), wspec(SSM_PAIR, sw), wspec(SSM_PAIR, sw),
                  wspec(sw, SSM_PAIR), wspec(sw, SSM_PAIR),
                  pl.BlockSpec((1, pps * sw), lambda p, b, c: (0, p)),
                  pl.BlockSpec((1, pps * sw), lambda p, b, c: (0, p)),
                  pl.BlockSpec((1, pps * SSM_PAIR), lambda p, b, c: (0, p))],
        out_specs=[u_spec, st_spec, st_spec],
        out_shape=[jax.ShapeDtypeStruct((nc, bsz, width), F32),
                   jax.ShapeDtypeStruct(h0_re.shape, F32),
                   jax.ShapeDtypeStruct(h0_im.shape, F32)],
        scratch_shapes=[pltpu.VMEM((lc * bt, pps * sw), F32) for _ in range(4)],
        compiler_params=_params("parallel", "parallel", "arbitrary"),
        name="ssm_scan",
    )(u_chunks, h0_re, h0_im, w['m'].astype(cdt), w['wb_re'].astype(cdt), w['wb_im'].astype(cdt),
      w['wc_re'].astype(cdt), w['wc_im'].astype(cdt), w['a_re'], w['a_im'], w['d'])


def _to_chunks(u, bsz, t):
    nc = t // SSM_CHUNK
    u = u.reshape(bsz, nc, SSM_CHUNK, SSM_GROUPS, SSM_GROUP)
    return jnp.transpose(u, (1, 0, 3, 2, 4)).reshape(nc, bsz, SSM_GROUPS * SSM_CHUNK * SSM_GROUP)


def _from_chunks(z, bsz, t):
    nc = t // SSM_CHUNK
    z = z.reshape(nc, bsz, SSM_GROUPS, SSM_CHUNK, SSM_GROUP)
    return jnp.transpose(z, (1, 0, 3, 2, 4)).reshape(bsz * t, SSM_WIDTH)


def _glu_kernel(z_ref, w_ref, b_ref, o_ref):
    z = z_ref[...]
    gate = jnp.dot(z.astype(BF16), w_ref[...], preferred_element_type=F32) + b_ref[...]
    o_ref[...] = (z * _sigmoid(gate)).astype(o_ref.dtype)


def _glu(z, w_bf16, b, tm=512):
    m = z.shape[0]
    return pl.pallas_call(
        _glu_kernel,
        grid=(m // tm,),
        in_specs=[pl.BlockSpec((tm, SSM_WIDTH), lambda i: (i, 0)),
                  pl.BlockSpec((SSM_WIDTH, SSM_WIDTH), lambda i: (0, 0)),
                  pl.BlockSpec((1, SSM_WIDTH), lambda i: (0, 0))],
        out_specs=pl.BlockSpec((tm, SSM_WIDTH), lambda i: (i, 0)),
        out_shape=jax.ShapeDtypeStruct((m, SSM_WIDTH), BF16),
        compiler_params=_params("parallel"),
        name="ssm_glu",
    )(z, w_bf16, b.reshape(1, SSM_WIDTH))


def _out_proj_kernel(attn_ref, ssm_ref, x_ref, w_ref, g_ref, h_ref, xn_ref):
    mix = (jnp.dot(attn_ref[...].astype(BF16), w_ref[:ATTN_WIDTH, :], preferred_element_type=F32)
           + jnp.dot(ssm_ref[...], w_ref[ATTN_WIDTH:, :], preferred_element_type=F32))
    h = x_ref[...] + mix
    h_ref[...] = h
    xn_ref[...] = _rms(h, g_ref[...]).astype(BF16)


def _out_proj(attn, ssm_out, x, w_bf16, g, tm=256):
    m = x.shape[0]
    return pl.pallas_call(
        _out_proj_kernel,
        grid=(m // tm,),
        in_specs=[pl.BlockSpec((tm, ATTN_WIDTH), lambda i: (i, 0)),
                  pl.BlockSpec((tm, SSM_WIDTH), lambda i: (i, 0)),
                  pl.BlockSpec((tm, D_MODEL), lambda i: (i, 0)),
                  pl.BlockSpec((D_MODEL, D_MODEL), lambda i: (0, 0)),
                  pl.BlockSpec((1, D_MODEL), lambda i: (0, 0))],
        out_specs=[pl.BlockSpec((tm, D_MODEL), lambda i: (i, 0)),
                   pl.BlockSpec((tm, D_MODEL), lambda i: (i, 0))],
        out_shape=[jax.ShapeDtypeStruct((m, D_MODEL), F32),
                   jax.ShapeDtypeStruct((m, D_MODEL), BF16)],
        compiler_params=_params("parallel"),
        name="out_proj",
    )(attn, ssm_out, x, w_bf16, g.reshape(1, D_MODEL))


def _top16(s):
    iota = lax.broadcasted_iota(jnp.int32, s.shape, 0)
    rank = jnp.full(s.shape, 31, jnp.int32)
    vals = []
    for r in range(PEER_TOPK):
        mx = jnp.max(s, axis=0, keepdims=True)
        idx = jnp.min(jnp.where(s == mx, iota, s.shape[0]), axis=0, keepdims=True)
        hit = iota == idx
        rank = jnp.where(hit, r, rank)
        s = jnp.where(hit, -jnp.inf, s)
        vals.append(mx)
    return vals, rank


def _peer_route_kernel(xn_ref, wq_ref, keys_ref, lena_ref, rkb_ref, ea_ref, eb_ref,
                       q_scr, cand_scr, sel_scr):
    tm = xn_ref.shape[0]
    q_scr[...] = jnp.dot(xn_ref[...], wq_ref[...], preferred_element_type=F32).astype(BF16)
    cand_scr[...] = jnp.full_like(cand_scr, -jnp.inf)
    nt = (((1,), (1,)), ((), ()))

    def head(idx, _):
        h = idx // (tm // LANES)
        sub = idx % (tm // LANES)
        r0 = pl.multiple_of(sub * LANES, LANES)
        c0 = pl.multiple_of(h * 2 * PEER_HALF, 2 * PEER_HALF)
        qa = q_scr[pl.ds(r0, LANES), pl.ds(c0, PEER_HALF)]
        qb = q_scr[pl.ds(r0, LANES), pl.ds(c0 + PEER_HALF, PEER_HALF)]
        s_a = lax.dot_general(keys_ref[0], qa, nt, preferred_element_type=F32)
        s_b = lax.dot_general(keys_ref[1], qb, nt, preferred_element_type=F32)
        va, ra = _top16(s_a)
        vb, rb = _top16(s_b)
        for p, (i, j) in enumerate(_PEER_CAND):
            cand_scr[p:p + 1, :] = va[i] + vb[j]
        cand = cand_scr[...]
        iota = lax.broadcasted_iota(jnp.int32, cand.shape, 0)
        work = cand
        sel = jnp.zeros(cand.shape, F32)
        for _r in range(PEER_TOPK):
            mx = jnp.max(work, axis=0, keepdims=True)
            pos = jnp.min(jnp.where(work == mx, iota, cand.shape[0]), axis=0, keepdims=True)
            hit = iota == pos
            sel = jnp.where(hit, 1.0, sel)
            work = jnp.where(hit, -jnp.inf, work)
        top = va[0] + vb[0]
        z = jnp.sum(jnp.where(sel > 0, jnp.exp(cand - top), 0.0), axis=0, keepdims=True)
        sel_scr[...] = sel
        lena = jnp.zeros(s_a.shape, F32)
        p = 0
        for i in range(PEER_TOPK):
            n_i = PEER_TOPK // (i + 1)
            len_i = jnp.sum(sel_scr[p:p + n_i, :], axis=0, keepdims=True)
            lena = jnp.where(ra == i, len_i, lena)
            p += n_i
        lena_ref[h, :, pl.ds(r0, LANES)] = lena
        rkb_ref[h, :, pl.ds(r0, LANES)] = rb.astype(F32).astype(BF16)
        ea_ref[h, :, pl.ds(r0, LANES)] = jnp.exp(s_a - va[0]) * (1.0 / z)
        eb_ref[h, :, pl.ds(r0, LANES)] = jnp.exp(s_b - vb[0]).astype(BF16)
        return 0

    lax.fori_loop(0, PEER_HEADS * (tm // LANES), head, 0)


def _peer_route(xn, wq_bf16, keys_bf16, tm=512):
    m = xn.shape[0]
    out_spec = pl.BlockSpec((PEER_HEADS, PEER_NKEYS, tm), lambda i: (0, 0, i))
    out_sds = [jax.ShapeDtypeStruct((PEER_HEADS, PEER_NKEYS, m), dt) for dt in (F32, BF16, F32, BF16)]
    return pl.pallas_call(
        _peer_route_kernel,
        grid=(m // tm,),
        in_specs=[pl.BlockSpec((tm, D_MODEL), lambda i: (i, 0)),
                  pl.BlockSpec((D_MODEL, D_MODEL), lambda i: (0, 0)),
                  pl.BlockSpec((2, PEER_NKEYS, PEER_HALF), lambda i: (0, 0, 0))],
        out_specs=[out_spec] * 4,
        out_shape=out_sds,
        scratch_shapes=[pltpu.VMEM((tm, D_MODEL), BF16),
                        pltpu.VMEM((_PEER_CAND_ROWS, LANES), F32),
                        pltpu.VMEM((_PEER_CAND_ROWS, LANES), F32)],
        compiler_params=_params("parallel"),
        name="peer_route",
    )(xn, wq_bf16, keys_bf16)


def _peer_dense_kernel(xn_ref, u_ref, v_ref, lena_ref, rkb_ref, ea_ref, eb_ref, o_ref,
                       pre0, pre1, ht0, ht1, *, ta, nj):
    j = pl.program_id(1)

    @pl.when(j == 0)
    def _():
        for ref in (pre0, pre1, ht0, ht1, o_ref):
            ref[...] = jnp.zeros_like(ref)

    def step(pre_w, pre_r, ht_w, ht_r):
        o_ref[...] += lax.dot_general(ht_r[...], v_ref[...], (((0,), (0,)), ((), ())),
                                      preferred_element_type=F32)
        tile = jnp.clip(j - 1, 0, nj - 1)
        for t in range(ta):
            a = tile * ta + t
            rows = slice(t * PEER_NKEYS, (t + 1) * PEER_NKEYS)
            w = None
            for h in range(PEER_HEADS):
                lena = lena_ref[h, pl.ds(a, 1), :].astype(BF16)
                ea = ea_ref[h, pl.ds(a, 1), :].astype(BF16)
                term = jnp.where(rkb_ref[h] < lena, eb_ref[h], jnp.zeros((), BF16)) * ea
                w = term if w is None else w + term
            ht_w[rows, :] = w * _gelu(pre_r[rows, :]).astype(BF16)
        pre_w[...] = lax.dot_general(u_ref[...], xn_ref[...], (((1,), (1,)), ((), ())),
                                     preferred_element_type=F32)

    @pl.when(j % 2 == 0)
    def _():
        step(pre0, pre1, ht1, ht0)

    @pl.when(j % 2 == 1)
    def _():
        step(pre1, pre0, ht0, ht1)


def _peer_dense(xn, u_bf16, v_bf16, route, tm=512, ta=4):
    m = xn.shape[0]
    nj = PEER_NKEYS // ta
    r_spec = pl.BlockSpec((PEER_HEADS, PEER_NKEYS, tm), lambda i, j: (0, 0, i))
    rows = ta * PEER_NKEYS
    return pl.pallas_call(
        functools.partial(_peer_dense_kernel, ta=ta, nj=nj),
        grid=(m // tm, nj + 2),
        in_specs=[pl.BlockSpec((tm, D_MODEL), lambda i, j: (i, 0)),
                  pl.BlockSpec((rows, D_MODEL), lambda i, j: (jnp.minimum(j, nj - 1), 0)),
                  pl.BlockSpec((rows, D_MODEL), lambda i, j: (jnp.clip(j - 2, 0, nj - 1), 0)),
                  r_spec, r_spec, r_spec, r_spec],
        out_specs=pl.BlockSpec((tm, D_MODEL), lambda i, j: (i, 0)),
        out_shape=jax.ShapeDtypeStruct((m, D_MODEL), F32),
        scratch_shapes=[pltpu.VMEM((rows, tm), F32), pltpu.VMEM((rows, tm), F32),
                        pltpu.VMEM((rows, tm), BF16), pltpu.VMEM((rows, tm), BF16)],
        compiler_params=_params("parallel", "arbitrary"),
        name="peer_dense",
    )(xn, u_bf16, v_bf16, *route)


def _ple_kernel(h_ref, peer_ref, ple_ref, pw_ref, gw_ref, g_ref, o_ref, *, final_norm):
    h = h_ref[...] + peer_ref[...]
    gate = _sigmoid(jnp.dot(h.astype(BF16), gw_ref[...], preferred_element_type=F32))
    emb = jnp.dot(ple_ref[...].astype(BF16), pw_ref[...], preferred_element_type=F32)
    h = h + emb * gate
    if final_norm:
        h = _rms(h, g_ref[...])
    o_ref[...] = h


def _ple(h, peer_out, ple, pw_bf16, gw_bf16, g, final_norm, tm=256):
    m = h.shape[0]
    row = pl.BlockSpec((tm, D_MODEL), lambda i: (i, 0))
    return pl.pallas_call(
        functools.partial(_ple_kernel, final_norm=final_norm),
        grid=(m // tm,),
        in_specs=[row, row,
                  pl.BlockSpec((tm, PLE_DIM), lambda i: (i, 0)),
                  pl.BlockSpec((PLE_DIM, D_MODEL), lambda i: (0, 0)),
                  pl.BlockSpec((D_MODEL, D_MODEL), lambda i: (0, 0)),
                  pl.BlockSpec((1, D_MODEL), lambda i: (0, 0))],
        out_specs=row,
        out_shape=jax.ShapeDtypeStruct((m, D_MODEL), F32),
        compiler_params=_params("parallel"),
        name="ple_gate",
    )(h, peer_out, ple, pw_bf16, gw_bf16, g.reshape(1, D_MODEL))


def kernel(x_prompt, x_sample, cache_k, cache_v, state_ssm_re, state_ssm_im, page_table, p_prompt, p_sample, norm_mix_g, norm_ffn_g, w_in, w_out, lam_q1, lam_k1, lam_q2, lam_k2, subln_g, rel_bias, ssm_a_re, ssm_a_im, ssm_b_re, ssm_b_im, ssm_c_re, ssm_c_im, ssm_d, ssm_log_dt, ssm_w_glu, ssm_b_glu, peer_w_q, peer_keys, peer_u, peer_v, ple_w, ple_gate_w, final_norm_g):
    bsz, seq = x_prompt.shape[:2]
    dec_batch, dec_seq = x_sample.shape[:2]
    n_pool = cache_k.shape[1]
    n_pages = page_table.shape[1]
    tq = 256
    nq = seq // tq

    bias_prompt = _prompt_bias_tiles(rel_bias, nq, tq)
    bias_pages, bias_new = _sample_bias_tables(rel_bias, n_pages, dec_seq)

    cache_k = cache_k.reshape(DEPTH, n_pool, PAGE_ROWS * N_HEADS, VD)
    cache_v = cache_v.reshape(DEPTH, n_pool, PAGE_ROWS * N_HEADS, VD)
    zeros_state = jnp.zeros((bsz, SSM_GROUPS * SSM_STATE), F32)

    hp = x_prompt.reshape(bsz * seq, D_MODEL)
    hs = x_sample.reshape(dec_batch * dec_seq, D_MODEL)
    outs = {k: [] for k in ('kp', 'vp', 'spr', 'spi', 'ks', 'vs', 'ssr', 'ssi')}

    for i in range(DEPTH):
        lam_init = 0.8 - 0.6 * math.exp(-0.3 * i)
        lam_params = jnp.stack([lam_q1[i], lam_k1[i], lam_q2[i], lam_k2[i]]).astype(F32)
        w_in_b = w_in[i].astype(BF16)
        w_out_b = w_out[i].astype(BF16)
        w_glu_b = ssm_w_glu[i].astype(BF16)
        wq_b = peer_w_q[i].astype(BF16)
        keys_b = peer_keys[i].astype(BF16)
        pu_b = peer_u[i].astype(BF16)
        pv_b = peer_v[i].astype(BF16)
        plw_b = ple_w[i].astype(BF16)
        gw_b = ple_gate_w[i].astype(BF16)
        ssm_w = _ssm_weights(ssm_a_re[i], ssm_a_im[i], ssm_b_re[i], ssm_b_im[i], ssm_c_re[i], ssm_c_im[i],
                             ssm_d[i], ssm_log_dt[i])
        last = i == DEPTH - 1

        def tail(h_in, attn, ssm_z, ple):
            ssm_out = _glu(ssm_z, w_glu_b, ssm_b_glu[i])
            h_mid, xn = _out_proj(attn, ssm_out, h_in, w_out_b, norm_ffn_g[i])
            route = _peer_route(xn, wq_b, keys_b)
            peer_out = _peer_dense(xn, pu_b, pv_b, route)
            return _ple(h_mid, peer_out, ple, plw_b, gw_b, final_norm_g, last)

        q, k_new, v_new, u = _in_proj(hp, norm_mix_g[i], w_in_b, BF16)
        attn = _attn_prompt(q, k_new, v_new, bias_prompt, lam_params, subln_g[i], lam_init, bsz, seq, tq)
        z, s_re, s_im = _ssm(_to_chunks(u, bsz, seq), zeros_state, zeros_state, ssm_w,
                             lc=min(128, seq // SSM_CHUNK), bt=8, cdt=BF16, precision=None)
        hp = tail(hp, attn, _from_chunks(z, bsz, seq), p_prompt[i].reshape(bsz * seq, PLE_DIM))
        outs['kp'].append(k_new.reshape(bsz, seq, N_HEADS, VD))
        outs['vp'].append(v_new.reshape(bsz, seq, N_HEADS, VD))
        outs['spr'].append(s_re.reshape(bsz, SSM_GROUPS, SSM_STATE))
        outs['spi'].append(s_im.reshape(bsz, SSM_GROUPS, SSM_STATE))

        q, k_new, v_new, u = _in_proj(hs, norm_mix_g[i], w_in_b, F32)
        attn = _attn_sample(i, page_table, q, k_new, v_new, cache_k, cache_v, bias_pages, bias_new,
                            lam_params, subln_g[i], lam_init, dec_batch, dec_seq)
        z, s_re, s_im = _ssm(_to_chunks(u, dec_batch, dec_seq),
                             state_ssm_re[i].astype(F32).reshape(dec_batch, -1),
                             state_ssm_im[i].astype(F32).reshape(dec_batch, -1), ssm_w,
                             lc=1, bt=dec_batch, cdt=F32, precision=lax.Precision.HIGHEST)
        hs = tail(hs, attn, _from_chunks(z, dec_batch, dec_seq), p_sample[i].reshape(dec_batch * dec_seq, PLE_DIM))
        outs['ks'].append(k_new.reshape(dec_batch, dec_seq, N_HEADS, VD))
        outs['vs'].append(v_new.reshape(dec_batch, dec_seq, N_HEADS, VD))
        outs['ssr'].append(s_re.reshape(dec_batch, SSM_GROUPS, SSM_STATE))
        outs['ssi'].append(s_im.reshape(dec_batch, SSM_GROUPS, SSM_STATE))

    return (hp.reshape(bsz, seq, D_MODEL), hs.reshape(dec_batch, dec_seq, D_MODEL),
            jnp.stack(outs['kp']), jnp.stack(outs['vp']), jnp.stack(outs['spr']), jnp.stack(outs['spi']),
            jnp.stack(outs['ks']), jnp.stack(outs['vs']), jnp.stack(outs['ssr']), jnp.stack(outs['ssi']))
```

```python
import functools
import math

import jax
import jax.numpy as jnp
from jax import lax
from jax.experimental import pallas as pl
from jax.experimental.pallas import tpu as pltpu

F32 = jnp.float32
BF16 = jnp.bfloat16

D_MODEL = 2048
DEPTH = 2
N_HEADS = 8
DK = 64
VD = 128
ATTN_WIDTH = N_HEADS * VD
SSM_WIDTH = 1024
SSM_GROUP = 16
SSM_GROUPS = 64
SSM_STATE = 64
N_BUCKETS = 32
MAX_DISTANCE = 128
PEER_HEADS = 8
PEER_NKEYS = 128
PEER_TOPK = 16
PEER_HALF = 128
PLE_DIM = 256
RMS_EPS = 1e-6
NEG_INF = -1e30

LANES = 128
SSM_CHUNK = 8
SSM_PAIR = 2 * SSM_CHUNK * SSM_GROUP
SSM_PAIRS = SSM_GROUPS // 2
SSM_PAIRS_PER_STEP = 4
VMEM_LIMIT = 50 * 1024 * 1024

_PEER_CAND = [(i, j) for i in range(PEER_TOPK) for j in range(PEER_TOPK) if (i + 1) * (j + 1) <= PEER_TOPK]
_PEER_CAND_ROWS = 56


def _params(*sem, flags=None):
    return pltpu.CompilerParams(dimension_semantics=sem, vmem_limit_bytes=VMEM_LIMIT, flags=flags)


def _gelu(x):
    c = math.sqrt(2.0 / math.pi)
    return (0.5 * x) * (1.0 + jnp.tanh(x * (c + (c * 0.044715) * (x * x))))


def _sigmoid(x):
    return 1.0 / (1.0 + jnp.exp(-x))


def _rms(x, g):
    return x * lax.rsqrt(jnp.mean(x * x, axis=-1, keepdims=True) + RMS_EPS) * g


def _in_proj_kernel(x_ref, g_ref, w_ref, q_ref, k_ref, v_ref, u_ref, xn_ref):
    j = pl.program_id(1)

    @pl.when(j == 0)
    def _():
        xn_ref[...] = _rms(x_ref[...], g_ref[...]).astype(BF16)

    acc = jnp.dot(xn_ref[...], w_ref[...], preferred_element_type=F32)

    @pl.when(j == 0)
    def _():
        q_ref[...] = (acc * DK ** -0.5).astype(q_ref.dtype)

    @pl.when(j == 1)
    def _():
        k_ref[...] = acc

    @pl.when(j == 2)
    def _():
        v_ref[...] = acc

    @pl.when(j == 3)
    def _():
        u_ref[...] = acc


def _in_proj(x, g, w_bf16, q_dtype, tm=512):
    m = x.shape[0]
    slab = pl.BlockSpec((tm, ATTN_WIDTH), lambda i, j: (i, 0))
    return pl.pallas_call(
        _in_proj_kernel,
        grid=(m // tm, 4),
        in_specs=[pl.BlockSpec((tm, D_MODEL), lambda i, j: (i, 0)),
                  pl.BlockSpec((1, D_MODEL), lambda i, j: (0, 0)),
                  pl.BlockSpec((D_MODEL, ATTN_WIDTH), lambda i, j: (0, j))],
        out_specs=[slab, slab, slab, slab],
        out_shape=[jax.ShapeDtypeStruct((m, ATTN_WIDTH), q_dtype),
                   jax.ShapeDtypeStruct((m, ATTN_WIDTH), F32),
                   jax.ShapeDtypeStruct((m, ATTN_WIDTH), F32),
                   jax.ShapeDtypeStruct((m, SSM_WIDTH), F32)],
        scratch_shapes=[pltpu.VMEM((tm, D_MODEL), BF16)],
        compiler_params=_params("parallel", "arbitrary"),
        name="in_proj",
    )(x, g.reshape(1, D_MODEL), w_bf16)


def _lam_value(lam_ref, lam_init):
    lp = lam_ref[...]
    l1 = jnp.sum(lp[0:1] * lp[1:2], axis=-1, keepdims=True)
    l2 = jnp.sum(lp[2:3] * lp[3:4], axis=-1, keepdims=True)
    return jnp.exp(l1) - jnp.exp(l2) + lam_init


def _split_maps(q):
    lane = lax.broadcasted_iota(jnp.int32, q.shape, 1)
    zero = jnp.zeros_like(q)
    return jnp.concatenate([jnp.where(lane < DK, q, zero), jnp.where(lane >= DK, q, zero)], axis=0)


def _bias_by_distance(rel_bias, lo, hi):
    rel = jnp.arange(lo, hi)
    n = jnp.maximum(rel, 0)
    max_exact = N_BUCKETS // 2
    nf = jnp.maximum(n, max_exact).astype(F32)
    large = max_exact + (jnp.log(nf / max_exact) / math.log(MAX_DISTANCE / max_exact)
                         * (N_BUCKETS - max_exact)).astype(jnp.int32)
    large = jnp.minimum(large, N_BUCKETS - 1)
    bucket = jnp.where(n < max_exact, n, large)
    onehot = bucket[:, None] == jnp.arange(N_BUCKETS)[None, :]
    bias = jnp.sum(jnp.where(onehot[None], rel_bias.astype(F32).T[:, None, :], 0.0), axis=-1)
    return jnp.where(rel >= 0, bias, NEG_INF)


def _prompt_bias_tiles(rel_bias, nq, tq):
    val = _bias_by_distance(rel_bias, -tq, nq * tq).reshape(N_HEADS, nq + 1, tq)
    v = jnp.concatenate([val[:, 1:], val[:, :-1]], axis=-1)
    x = jnp.broadcast_to(v[:, :, None, :], (N_HEADS, nq, tq, 2 * tq)).reshape(N_HEADS, nq, 2 * tq * tq)
    x = x[:, :, :tq * (2 * tq - 1)].reshape(N_HEADS, nq, tq, 2 * tq - 1)
    return x[:, :, :, :tq]


def _sample_bias_tables(rel_bias, n_pages, dec_seq):
    past_len = n_pages * PAGE_ROWS
    same_head = jnp.eye(N_HEADS, dtype=bool)
    val = _bias_by_distance(rel_bias, 0, past_len + dec_seq)
    bp = jnp.stack([val[:, t + 1:past_len + t + 1][:, ::-1] for t in range(dec_seq)], axis=1)
    bp = jnp.transpose(bp.reshape(N_HEADS, dec_seq, n_pages, PAGE_ROWS), (2, 0, 1, 3))
    bp = jnp.broadcast_to(bp[:, :, None], (n_pages, N_HEADS, 2, dec_seq, PAGE_ROWS))
    pages = jnp.where(same_head[None, :, None, None, None, :], bp[..., None], NEG_INF)
    pages = pages.reshape(n_pages, N_HEADS * 2 * dec_seq, PAGE_ROWS * N_HEADS)
    valn = _bias_by_distance(rel_bias, 1 - dec_seq, dec_seq)
    t_i, s_i = jnp.meshgrid(jnp.arange(dec_seq), jnp.arange(dec_seq), indexing='ij')
    bn = jnp.take(valn, t_i - s_i + dec_seq - 1, axis=1)
    bn = jnp.broadcast_to(bn[:, None], (N_HEADS, 2, dec_seq, dec_seq))
    new = jnp.where(same_head[:, None, None, None, :], bn[..., None], NEG_INF)
    new = new.reshape(N_HEADS * 2 * dec_seq, dec_seq * N_HEADS)
    new = jnp.pad(new, ((0, 0), (0, PAGE_ROWS - dec_seq * N_HEADS)), constant_values=NEG_INF)
    return pages, new


ATTN_HEADS_PER_STEP = 4


def _attn_prompt_kernel(lam_ref, q_ref, k_ref, v_ref, bias_ref, g_ref, o_ref, acc_scr, *, tq, lam_init):
    qi = pl.program_id(2)
    heads = range(ATTN_HEADS_PER_STEP)
    qq = [_split_maps(q_ref[:, h * VD:(h + 1) * VD]) for h in heads]
    acc_scr[...] = jnp.zeros_like(acc_scr)

    def body(kj, carry):
        start = pl.multiple_of(kj * tq, tq)
        ss = []
        for h in heads:
            kb = k_ref[pl.ds(start, tq), h * VD:(h + 1) * VD].astype(BF16)
            ss.append(lax.dot_general(kb, qq[h], (((1,), (1,)), ((), ())), preferred_element_type=F32))
        out, ps, alphas = [], [], []
        for h in heads:
            m_old, l_old = carry[h]
            bias = bias_ref[h, qi - kj]
            s = ss[h] + jnp.concatenate([bias, bias], axis=1)
            m_new = jnp.maximum(m_old, jnp.max(s, axis=0, keepdims=True))
            alpha = jnp.exp(m_old - m_new)
            p = jnp.exp(s - m_new)
            out.append((m_new, alpha * l_old + jnp.sum(p, axis=0, keepdims=True)))
            ps.append(p.astype(BF16))
            alphas.append(alpha)
        for h in heads:
            vb = v_ref[pl.ds(start, tq), h * VD:(h + 1) * VD].astype(BF16)
            pv = lax.dot_general(vb, ps[h], (((0,), (0,)), ((), ())), preferred_element_type=F32)
            acc_scr[h] = alphas[h] * acc_scr[h] + pv
        return tuple(out)

    m0 = jnp.full((1, 2 * tq), NEG_INF, F32)
    l0 = jnp.zeros((1, 2 * tq), F32)
    fin = lax.fori_loop(0, qi + 1, body, tuple((m0, l0) for _ in heads))

    lam = _lam_value(lam_ref, lam_init)
    for h in heads:
        acc = acc_scr[h] / fin[h][1]
        o = acc[:, :tq] - lam * acc[:, tq:]
        o = o * lax.rsqrt(jnp.mean(o * o, axis=0, keepdims=True) + RMS_EPS)
        o = o * g_ref[...] * (1.0 - lam_init)
        o_ref[:, h * VD:(h + 1) * VD] = o.T.astype(o_ref.dtype)


def _attn_prompt(q, k, v, bias_t, lam_params, subln_g, lam_init, bsz, seq, tq=256):
    nq = seq // tq
    hps = ATTN_HEADS_PER_STEP
    kv_spec = pl.BlockSpec((seq, hps * VD), lambda b, h, i: (b, h))
    return pl.pallas_call(
        functools.partial(_attn_prompt_kernel, tq=tq, lam_init=lam_init),
        grid=(bsz, N_HEADS // hps, nq),
        in_specs=[pl.BlockSpec((4, DK), lambda b, h, i: (0, 0)),
                  pl.BlockSpec((tq, hps * VD), lambda b, h, i: (b * nq + i, h)),
                  kv_spec, kv_spec,
                  pl.BlockSpec((hps, nq, tq, tq), lambda b, h, i: (h, 0, 0, 0)),
                  pl.BlockSpec((VD, 1), lambda b, h, i: (0, 0))],
        out_specs=pl.BlockSpec((tq, hps * VD), lambda b, h, i: (b * nq + i, h)),
        out_shape=jax.ShapeDtypeStruct((bsz * seq, ATTN_WIDTH), BF16),
        scratch_shapes=[pltpu.VMEM((hps, VD, 2 * tq), F32)],
        compiler_params=_params("parallel", "parallel", "arbitrary"),
        name="attn_prompt",
    )(lam_params, q, k, v, bias_t, subln_g.reshape(VD, 1))


PAGE_ROWS = 128
SAMPLE_PAGES_PER_STEP = 4


def _attn_sample_kernel(pt_ref, lam_ref, q_ref, kn_ref, vn_ref, *rest, n_steps, pps, dec_seq, lam_init):
    del pt_ref
    kc_refs, vc_refs = rest[:pps], rest[pps:2 * pps]
    bias_ref, biasn_ref, g_ref, o_ref, q_scr, m_scr, l_scr, acc_scr = rest[2 * pps:]
    j = pl.program_id(1)
    nt = (((1,), (1,)), ((), ()))

    @pl.when(j == 0)
    def _():
        parts = [_split_maps(q_ref[:, h * VD:(h + 1) * VD]) for h in range(N_HEADS)]
        q_scr[...] = jnp.concatenate(parts, axis=0).astype(BF16)
        m_scr[...] = jnp.full_like(m_scr, NEG_INF)
        l_scr[...] = jnp.zeros_like(l_scr)
        acc_scr[...] = jnp.zeros_like(acc_scr)

    def update(kbs, vbs, biases):
        q = q_scr[...]
        ss = [lax.dot_general(q, kb, nt, preferred_element_type=F32) + b for kb, b in zip(kbs, biases)]
        m_old = m_scr[...]
        m_new = m_old
        for s in ss:
            m_new = jnp.maximum(m_new, jnp.max(s, axis=-1, keepdims=True))
        alpha = jnp.exp(m_old - m_new)
        l_new = alpha * l_scr[...]
        acc = alpha * acc_scr[...]
        for s, vb in zip(ss, vbs):
            p = jnp.exp(s - m_new)
            l_new = l_new + jnp.sum(p, axis=-1, keepdims=True)
            acc = acc + jnp.dot(p.astype(BF16), vb, preferred_element_type=F32)
        l_scr[...] = l_new
        acc_scr[...] = acc
        m_scr[...] = m_new

    update([r[...].astype(BF16) for r in kc_refs], [r[...].astype(BF16) for r in vc_refs],
           [bias_ref[j * pps + i] for i in range(pps)])

    @pl.when(j == n_steps - 1)
    def _():
        pad = jnp.zeros((PAGE_ROWS - dec_seq * N_HEADS, VD), F32)
        update([jnp.concatenate([kn_ref[...], pad], axis=0).astype(BF16)],
               [jnp.concatenate([vn_ref[...], pad], axis=0).astype(BF16)], [biasn_ref[...]])
        lam = _lam_value(lam_ref, lam_init)
        acc = acc_scr[...] / l_scr[...]
        for h in range(N_HEADS):
            r0 = h * 2 * dec_seq
            o = acc[r0:r0 + dec_seq] - lam * acc[r0 + dec_seq:r0 + 2 * dec_seq]
            o = o * lax.rsqrt(jnp.mean(o * o, axis=-1, keepdims=True) + RMS_EPS)
            o_ref[:, h * VD:(h + 1) * VD] = o * g_ref[...] * (1.0 - lam_init)


def _attn_sample(layer, page_table, q, k_new, v_new, cache_k, cache_v, bias_pages, bias_new,
                 lam_params, subln_g, lam_init, dec_batch, dec_seq):
    n_pages = page_table.shape[1]
    pps = math.gcd(n_pages, SAMPLE_PAGES_PER_STEP)
    rows = PAGE_ROWS * N_HEADS
    qrows = N_HEADS * 2 * dec_seq
    tok_spec = pl.BlockSpec((dec_seq, ATTN_WIDTH), lambda b, j, pt: (b, 0))
    new_spec = pl.BlockSpec((dec_seq * N_HEADS, VD), lambda b, j, pt: (b, 0))
    page_specs = [pl.BlockSpec((None, None, rows, VD), lambda b, j, pt, i=i: (layer, pt[b, j * pps + i], 0, 0))
                  for i in range(pps)]
    grid_spec = pltpu.PrefetchScalarGridSpec(
        num_scalar_prefetch=1,
        grid=(dec_batch, n_pages // pps),
        in_specs=[pl.BlockSpec((4, DK), lambda b, j, pt: (0, 0)),
                  tok_spec, new_spec, new_spec, *page_specs, *page_specs,
                  pl.BlockSpec((n_pages, qrows, rows), lambda b, j, pt: (0, 0, 0)),
                  pl.BlockSpec((qrows, PAGE_ROWS), lambda b, j, pt: (0, 0)),
                  pl.BlockSpec((1, VD), lambda b, j, pt: (0, 0))],
        out_specs=tok_spec,
        scratch_shapes=[pltpu.VMEM((qrows, VD), BF16),
                        pltpu.VMEM((qrows, 1), F32),
                        pltpu.VMEM((qrows, 1), F32),
                        pltpu.VMEM((qrows, VD), F32)])
    return pl.pallas_call(
        functools.partial(_attn_sample_kernel, n_steps=n_pages // pps, pps=pps, dec_seq=dec_seq,
                          lam_init=lam_init),
        grid_spec=grid_spec,
        out_shape=jax.ShapeDtypeStruct((dec_batch * dec_seq, ATTN_WIDTH), F32),
        compiler_params=_params("parallel", "arbitrary"),
        name="attn_sample",
    )(page_table, lam_params, q, k_new.reshape(-1, VD), v_new.reshape(-1, VD),
      *([cache_k] * pps), *([cache_v] * pps), bias_pages, bias_new, subln_g.reshape(1, VD))


def _ssm_weights(a_re, a_im, b_re, b_im, c_re, c_im, d, log_dt):
    hp = lax.Precision.HIGHEST
    g, n, p, c = SSM_GROUPS, SSM_STATE, SSM_GROUP, SSM_CHUNK
    dt = jnp.exp(log_dt.astype(F32))[:, None]
    a_re, a_im = a_re.astype(F32), a_im.astype(F32)
    tau = jnp.arange(c + 1, dtype=F32)[:, None, None]
    mag = jnp.exp(a_re * dt * tau)
    ang = a_im * dt * tau
    pw_re, pw_im = mag * jnp.cos(ang), mag * jnp.sin(ang)
    nr, ni = pw_re[1] - 1.0, pw_im[1]
    den = a_re * a_re + a_im * a_im
    f_re, f_im = (nr * a_re + ni * a_im) / den, (ni * a_re - nr * a_im) / den
    b_re, b_im = b_re.astype(F32), b_im.astype(F32)
    bb_re = f_re[:, :, None] * b_re - f_im[:, :, None] * b_im
    bb_im = f_re[:, :, None] * b_im + f_im[:, :, None] * b_re
    c_re, c_im = c_re.astype(F32), c_im.astype(F32)
    cp_re = c_re[None] * pw_re[:, :, None, :] - c_im[None] * pw_im[:, :, None, :]
    cp_im = c_re[None] * pw_im[:, :, None, :] + c_im[None] * pw_re[:, :, None, :]
    kern = (jnp.einsum('tgpn,gnq->tgpq', cp_re, bb_re, precision=hp)
            - jnp.einsum('tgpn,gnq->tgpq', cp_im, bb_im, precision=hp))
    ss, tt = jnp.meshgrid(jnp.arange(c), jnp.arange(c), indexing='ij')
    lag = tt - ss
    m_intra = jnp.where((lag >= 0)[:, :, None, None, None], kern[jnp.maximum(lag, 0)], 0.0)
    m_intra = jnp.transpose(m_intra, (2, 0, 4, 1, 3)).reshape(g, c * p, c * p)
    rev = pw_re[c - 1::-1][:c], pw_im[c - 1::-1][:c]
    wb_re = rev[0][:, :, :, None] * bb_re[None] - rev[1][:, :, :, None] * bb_im[None]
    wb_im = rev[0][:, :, :, None] * bb_im[None] + rev[1][:, :, :, None] * bb_re[None]
    wb_re = jnp.transpose(wb_re, (1, 0, 3, 2)).reshape(g, c * p, n)
    wb_im = jnp.transpose(wb_im, (1, 0, 3, 2)).reshape(g, c * p, n)
    wc_re = jnp.transpose(cp_re[1:], (1, 3, 0, 2)).reshape(g, n, c * p)
    wc_im = -jnp.transpose(cp_im[1:], (1, 3, 0, 2)).reshape(g, n, c * p)

    def pair_diag(x):
        r, cc = x.shape[1:]
        x = x.reshape(g // 2, 2, r, cc)
        return jnp.einsum('kirc,ij->kirjc', x, jnp.eye(2, dtype=F32)).reshape(g // 2, 2 * r, 2 * cc)

    d_chunk = jnp.broadcast_to(d.astype(F32).reshape(g, 1, p), (g, c, p)).reshape(1, g * c * p)
    return dict(m=pair_diag(m_intra), wb_re=pair_diag(wb_re), wb_im=pair_diag(wb_im),
                wc_re=pair_diag(wc_re), wc_im=pair_diag(wc_im),
                a_re=pw_re[c].reshape(1, g * n), a_im=pw_im[c].reshape(1, g * n), d=d_chunk)


def _ssm_kernel(u_ref, h0re_ref, h0im_ref, m_ref, wbre_ref, wbim_ref, wcre_ref, wcim_ref,
                are_ref, aim_ref, d_ref, z_ref, hre_ref, him_ref,
                sre_scr, sim_scr, hsre_scr, hsim_scr, *, lc, bt, cdt, precision):
    ci = pl.program_id(2)
    rows = lc * bt
    sw = 2 * SSM_STATE
    dot = functools.partial(jnp.dot, preferred_element_type=F32, precision=precision)

    @pl.when(ci == 0)
    def _():
        hre_ref[...] = h0re_ref[...]
        him_ref[...] = h0im_ref[...]

    for k in range(SSM_PAIRS_PER_STEP):
        ug = u_ref[:, :, k * SSM_PAIR:(k + 1) * SSM_PAIR].reshape(rows, SSM_PAIR).astype(cdt)
        sre_scr[:, k * sw:(k + 1) * sw] = dot(ug, wbre_ref[k])
        sim_scr[:, k * sw:(k + 1) * sw] = dot(ug, wbim_ref[k])

    a_re = are_ref[...]
    a_im = aim_ref[...]

    def step(c, carry):
        h_re, h_im = carry
        r0 = pl.multiple_of(c * bt, bt)
        hsre_scr[pl.ds(r0, bt), :] = h_re
        hsim_scr[pl.ds(r0, bt), :] = h_im
        n_re = a_re * h_re - a_im * h_im + sre_scr[pl.ds(r0, bt), :]
        n_im = a_re * h_im + a_im * h_re + sim_scr[pl.ds(r0, bt), :]
        return n_re, n_im

    h_re, h_im = lax.fori_loop(0, lc, step, (hre_ref[...], him_ref[...]))
    hre_ref[...] = h_re
    him_ref[...] = h_im

    for k in range(SSM_PAIRS_PER_STEP):
        lanes = slice(k * SSM_PAIR, (k + 1) * SSM_PAIR)
        ug = u_ref[:, :, lanes].reshape(rows, SSM_PAIR)
        y = (dot(ug.astype(cdt), m_ref[k])
             + dot(hsre_scr[:, k * sw:(k + 1) * sw].astype(cdt), wcre_ref[k])
             + dot(hsim_scr[:, k * sw:(k + 1) * sw].astype(cdt), wcim_ref[k])
             + d_ref[:, lanes] * ug)
        z_ref[:, :, lanes] = _gelu(y).reshape(lc, bt, SSM_PAIR)


def _ssm(u_chunks, h0_re, h0_im, w, lc, bt, cdt, precision):
    nc, bsz, width = u_chunks.shape
    pps = SSM_PAIRS_PER_STEP
    sw = 2 * SSM_STATE
    u_spec = pl.BlockSpec((lc, bt, pps * SSM_PAIR), lambda p, b, c: (c, b, p))
    st_spec = pl.BlockSpec((bt, pps * sw), lambda p, b, c: (b, p))

    def wspec(r, cc):
        return pl.BlockSpec((pps, r, cc), lambda p, b, c: (p, 0, 0))

    return pl.pallas_call(
        functools.partial(_ssm_kernel, lc=lc, bt=bt, cdt=cdt, precision=precision),
        grid=(SSM_PAIRS // pps, bsz // bt, nc // lc),
        in_specs=[u_spec, st_spec, st_spec,
                  wspec(SSM_PAIR, SSM_PAIR), wspec(SSM_PAIR, sw), wspec(SSM_PAIR, sw),
                  wspec(sw, SSM_PAIR), wspec(sw, SSM_PAIR),
                  pl.BlockSpec((1, pps * sw), lambda p, b, c: (0, p)),
                  pl.BlockSpec((1, pps * sw), lambda p, b, c: (0, p)),
                  pl.BlockSpec((1, pps * SSM_PAIR), lambda p, b, c: (0, p))],
        out_specs=[u_spec, st_spec, st_spec],
        out_shape=[jax.ShapeDtypeStruct((nc, bsz, width), F32),
                   jax.ShapeDtypeStruct(h0_re.shape, F32),
                   jax.ShapeDtypeStruct(h0_im.shape, F32)],
        scratch_shapes=[pltpu.VMEM((lc * bt, pps * sw), F32) for _ in range(4)],
        compiler_params=_params("parallel", "parallel", "arbitrary"),
        name="ssm_scan",
    )(u_chunks, h0_re, h0_im, w['m'].astype(cdt), w['wb_re'].astype(cdt), w['wb_im'].astype(cdt),
      w['wc_re'].astype(cdt), w['wc_im'].astype(cdt), w['a_re'], w['a_im'], w['d'])


def _to_chunks(u, bsz, t):
    nc = t // SSM_CHUNK
    u = u.reshape(bsz, nc, SSM_CHUNK, SSM_GROUPS, SSM_GROUP)
    return jnp.transpose(u, (1, 0, 3, 2, 4)).reshape(nc, bsz, SSM_GROUPS * SSM_CHUNK * SSM_GROUP)


def _from_chunks(z, bsz, t):
    nc = t // SSM_CHUNK
    z = z.reshape(nc, bsz, SSM_GROUPS, SSM_CHUNK, SSM_GROUP)
    return jnp.transpose(z, (1, 0, 3, 2, 4)).reshape(bsz * t, SSM_WIDTH)


def _glu_kernel(z_ref, w_ref, b_ref, o_ref):
    z = z_ref[...]
    gate = jnp.dot(z.astype(BF16), w_ref[...], preferred_element_type=F32) + b_ref[...]
    o_ref[...] = (z * _sigmoid(gate)).astype(o_ref.dtype)


def _glu(z, w_bf16, b, tm=512):
    m = z.shape[0]
    return pl.pallas_call(
        _glu_kernel,
        grid=(m // tm,),
        in_specs=[pl.BlockSpec((tm, SSM_WIDTH), lambda i: (i, 0)),
                  pl.BlockSpec((SSM_WIDTH, SSM_WIDTH), lambda i: (0, 0)),
                  pl.BlockSpec((1, SSM_WIDTH), lambda i: (0, 0))],
        out_specs=pl.BlockSpec((tm, SSM_WIDTH), lambda i: (i, 0)),
        out_shape=jax.ShapeDtypeStruct((m, SSM_WIDTH), BF16),
        compiler_params=_params("parallel"),
        name="ssm_glu",
    )(z, w_bf16, b.reshape(1, SSM_WIDTH))


def _out_proj_kernel(attn_ref, ssm_ref, x_ref, w_ref, g_ref, h_ref, xn_ref):
    mix = (jnp.dot(attn_ref[...].astype(BF16), w_ref[:ATTN_WIDTH, :], preferred_element_type=F32)
           + jnp.dot(ssm_ref[...], w_ref[ATTN_WIDTH:, :], preferred_element_type=F32))
    h = x_ref[...] + mix
    h_ref[...] = h
    xn_ref[...] = _rms(h, g_ref[...]).astype(BF16)


def _out_proj(attn, ssm_out, x, w_bf16, g, tm=256):
    m = x.shape[0]
    return pl.pallas_call(
        _out_proj_kernel,
        grid=(m // tm,),
        in_specs=[pl.BlockSpec((tm, ATTN_WIDTH), lambda i: (i, 0)),
                  pl.BlockSpec((tm, SSM_WIDTH), lambda i: (i, 0)),
                  pl.BlockSpec((tm, D_MODEL), lambda i: (i, 0)),
                  pl.BlockSpec((D_MODEL, D_MODEL), lambda i: (0, 0)),
                  pl.BlockSpec((1, D_MODEL), lambda i: (0, 0))],
        out_specs=[pl.BlockSpec((tm, D_MODEL), lambda i: (i, 0)),
                   pl.BlockSpec((tm, D_MODEL), lambda i: (i, 0))],
        out_shape=[jax.ShapeDtypeStruct((m, D_MODEL), F32),
                   jax.ShapeDtypeStruct((m, D_MODEL), BF16)],
        compiler_params=_params("parallel"),
        name="out_proj",
    )(attn, ssm_out, x, w_bf16, g.reshape(1, D_MODEL))


def _pick16(s, exact):
    iota = lax.broadcasted_iota(jnp.int32, s.shape, 0)
    rank = jnp.full(s.shape, 31, jnp.int32)
    vals = []
    for r in range(PEER_TOPK):
        mx = jnp.max(s, axis=0, keepdims=True)
        hit = s == mx
        if exact:
            hit = iota == jnp.min(jnp.where(hit, iota, s.shape[0]), axis=0, keepdims=True)
        rank = jnp.where(hit, r, rank)
        s = jnp.where(hit, -jnp.inf, s)
        vals.append(mx)
    return vals, rank


def _peer_route_kernel(xn_ref, wq_ref, keys_ref, lena_ref, rkb_ref, ea_ref, eb_ref,
                       q_scr, cand_scr, sel_scr):
    tm = xn_ref.shape[0]
    q_scr[...] = jnp.dot(xn_ref[...], wq_ref[...], preferred_element_type=F32).astype(BF16)
    cand_scr[...] = jnp.full_like(cand_scr, -jnp.inf)
    nt = (((1,), (1,)), ((), ()))

    def head(idx, _):
        h = idx // (tm // LANES)
        sub = idx % (tm // LANES)
        r0 = pl.multiple_of(sub * LANES, LANES)
        c0 = pl.multiple_of(h * 2 * PEER_HALF, 2 * PEER_HALF)
        qa = q_scr[pl.ds(r0, LANES), pl.ds(c0, PEER_HALF)]
        qb = q_scr[pl.ds(r0, LANES), pl.ds(c0 + PEER_HALF, PEER_HALF)]
        s_a = lax.dot_general(keys_ref[0], qa, nt, preferred_element_type=F32)
        s_b = lax.dot_general(keys_ref[1], qb, nt, preferred_element_type=F32)

        def route(exact):
            va, ra = _pick16(s_a, exact)
            vb, rb = _pick16(s_b, exact)
            for p, (i, j) in enumerate(_PEER_CAND):
                cand_scr[p:p + 1, :] = va[i] + vb[j]
            cand = cand_scr[...]
            _, rc = _pick16(cand, exact)
            sel = jnp.where(rc < 31, 1.0, 0.0)
            top = va[0] + vb[0]
            z = jnp.sum(sel * jnp.exp(cand - top), axis=0, keepdims=True)
            sel_scr[...] = sel
            lena = jnp.zeros(s_a.shape, F32)
            p = 0
            for i in range(PEER_TOPK):
                n_i = PEER_TOPK // (i + 1)
                len_i = jnp.sum(sel_scr[p:p + n_i, :], axis=0, keepdims=True)
                lena = jnp.where(ra == i, len_i, lena)
                p += n_i
            lena_ref[h, :, pl.ds(r0, LANES)] = lena
            rkb_ref[h, :, pl.ds(r0, LANES)] = rb.astype(F32).astype(BF16)
            ea_ref[h, :, pl.ds(r0, LANES)] = jnp.exp(s_a - va[0]) * (1.0 / z)
            eb_ref[h, :, pl.ds(r0, LANES)] = jnp.exp(s_b - vb[0]).astype(BF16)
            ranked = (jnp.sum(jnp.where(ra < 31, 1.0, 0.0), axis=0, keepdims=True)
                      + jnp.sum(jnp.where(rb < 31, 1.0, 0.0), axis=0, keepdims=True)
                      + jnp.sum(sel, axis=0, keepdims=True))
            return jnp.max(ranked)

        ranked = route(exact=False)

        @pl.when(ranked > 3 * PEER_TOPK)
        def _():
            route(exact=True)

        return 0

    lax.fori_loop(0, PEER_HEADS * (tm // LANES), head, 0)


def _peer_route(xn, wq_bf16, keys_bf16, tm=512):
    m = xn.shape[0]
    out_spec = pl.BlockSpec((PEER_HEADS, PEER_NKEYS, tm), lambda i: (0, 0, i))
    out_sds = [jax.ShapeDtypeStruct((PEER_HEADS, PEER_NKEYS, m), dt) for dt in (F32, BF16, F32, BF16)]
    return pl.pallas_call(
        _peer_route_kernel,
        grid=(m // tm,),
        in_specs=[pl.BlockSpec((tm, D_MODEL), lambda i: (i, 0)),
                  pl.BlockSpec((D_MODEL, D_MODEL), lambda i: (0, 0)),
                  pl.BlockSpec((2, PEER_NKEYS, PEER_HALF), lambda i: (0, 0, 0))],
        out_specs=[out_spec] * 4,
        out_shape=out_sds,
        scratch_shapes=[pltpu.VMEM((tm, D_MODEL), BF16),
                        pltpu.VMEM((_PEER_CAND_ROWS, LANES), F32),
                        pltpu.VMEM((_PEER_CAND_ROWS, LANES), F32)],
        compiler_params=_params("parallel"),
        name="peer_route",
    )(xn, wq_bf16, keys_bf16)


def _peer_dense_kernel(xn_ref, u_ref, v_ref, lena_ref, rkb_ref, ea_ref, eb_ref, o_ref,
                       pre0, pre1, ht0, ht1, *, ta, nj):
    j = pl.program_id(1)

    @pl.when(j == 0)
    def _():
        for ref in (pre0, pre1, ht0, ht1, o_ref):
            ref[...] = jnp.zeros_like(ref)

    def step(pre_w, pre_r, ht_w, ht_r):
        o_ref[...] += lax.dot_general(ht_r[...], v_ref[...], (((0,), (0,)), ((), ())),
                                      preferred_element_type=F32)
        tile = jnp.clip(j - 1, 0, nj - 1)
        for t in range(ta):
            a = tile * ta + t
            rows = slice(t * PEER_NKEYS, (t + 1) * PEER_NKEYS)
            w = None
            for h in range(PEER_HEADS):
                lena = lena_ref[h, pl.ds(a, 1), :].astype(BF16)
                ea = ea_ref[h, pl.ds(a, 1), :].astype(BF16)
                term = jnp.where(rkb_ref[h] < lena, eb_ref[h], jnp.zeros((), BF16)) * ea
                w = term if w is None else w + term
            ht_w[rows, :] = w * _gelu(pre_r[rows, :]).astype(BF16)
        pre_w[...] = lax.dot_general(u_ref[...], xn_ref[...], (((1,), (1,)), ((), ())),
                                     preferred_element_type=F32)

    @pl.when(j % 2 == 0)
    def _():
        step(pre0, pre1, ht1, ht0)

    @pl.when(j % 2 == 1)
    def _():
        step(pre1, pre0, ht0, ht1)


def _peer_dense(xn, u_bf16, v_bf16, route, tm=512, ta=4):
    m = xn.shape[0]
    nj = PEER_NKEYS // ta
    r_spec = pl.BlockSpec((PEER_HEADS, PEER_NKEYS, tm), lambda i, j: (0, 0, i))
    rows = ta * PEER_NKEYS
    return pl.pallas_call(
        functools.partial(_peer_dense_kernel, ta=ta, nj=nj),
        grid=(m // tm, nj + 2),
        in_specs=[pl.BlockSpec((tm, D_MODEL), lambda i, j: (i, 0)),
                  pl.BlockSpec((rows, D_MODEL), lambda i, j: (jnp.minimum(j, nj - 1), 0)),
                  pl.BlockSpec((rows, D_MODEL), lambda i, j: (jnp.clip(j - 2, 0, nj - 1), 0)),
                  r_spec, r_spec, r_spec, r_spec],
        out_specs=pl.BlockSpec((tm, D_MODEL), lambda i, j: (i, 0)),
        out_shape=jax.ShapeDtypeStruct((m, D_MODEL), F32),
        scratch_shapes=[pltpu.VMEM((rows, tm), F32), pltpu.VMEM((rows, tm), F32),
                        pltpu.VMEM((rows, tm), BF16), pltpu.VMEM((rows, tm), BF16)],
        compiler_params=_params("parallel", "arbitrary"),
        name="peer_dense",
    )(xn, u_bf16, v_bf16, *route)


def _ple_kernel(h_ref, peer_ref, ple_ref, pw_ref, gw_ref, g_ref, o_ref, *, final_norm):
    h = h_ref[...] + peer_ref[...]
    gate = _sigmoid(jnp.dot(h.astype(BF16), gw_ref[...], preferred_element_type=F32))
    emb = jnp.dot(ple_ref[...].astype(BF16), pw_ref[...], preferred_element_type=F32)
    h = h + emb * gate
    if final_norm:
        h = _rms(h, g_ref[...])
    o_ref[...] = h


def _ple(h, peer_out, ple, pw_bf16, gw_bf16, g, final_norm, tm=256):
    m = h.shape[0]
    row = pl.BlockSpec((tm, D_MODEL), lambda i: (i, 0))
    return pl.pallas_call(
        functools.partial(_ple_kernel, final_norm=final_norm),
        grid=(m // tm,),
        in_specs=[row, row,
                  pl.BlockSpec((tm, PLE_DIM), lambda i: (i, 0)),
                  pl.BlockSpec((PLE_DIM, D_MODEL), lambda i: (0, 0)),
                  pl.BlockSpec((D_MODEL, D_MODEL), lambda i: (0, 0)),
                  pl.BlockSpec((1, D_MODEL), lambda i: (0, 0))],
        out_specs=row,
        out_shape=jax.ShapeDtypeStruct((m, D_MODEL), F32),
        compiler_params=_params("parallel"),
        name="ple_gate",
    )(h, peer_out, ple, pw_bf16, gw_bf16, g.reshape(1, D_MODEL))


def kernel(x_prompt, x_sample, cache_k, cache_v, state_ssm_re, state_ssm_im, page_table, p_prompt, p_sample, norm_mix_g, norm_ffn_g, w_in, w_out, lam_q1, lam_k1, lam_q2, lam_k2, subln_g, rel_bias, ssm_a_re, ssm_a_im, ssm_b_re, ssm_b_im, ssm_c_re, ssm_c_im, ssm_d, ssm_log_dt, ssm_w_glu, ssm_b_glu, peer_w_q, peer_keys, peer_u, peer_v, ple_w, ple_gate_w, final_norm_g):
    bsz, seq = x_prompt.shape[:2]
    dec_batch, dec_seq = x_sample.shape[:2]
    n_pool = cache_k.shape[1]
    n_pages = page_table.shape[1]
    tq = 256
    nq = seq // tq

    bias_prompt = _prompt_bias_tiles(rel_bias, nq, tq)
    bias_pages, bias_new = _sample_bias_tables(rel_bias, n_pages, dec_seq)

    cache_k = cache_k.reshape(DEPTH, n_pool, PAGE_ROWS * N_HEADS, VD)
    cache_v = cache_v.reshape(DEPTH, n_pool, PAGE_ROWS * N_HEADS, VD)
    zeros_state = jnp.zeros((bsz, SSM_GROUPS * SSM_STATE), F32)

    hp = x_prompt.reshape(bsz * seq, D_MODEL)
    hs = x_sample.reshape(dec_batch * dec_seq, D_MODEL)
    outs = {k: [] for k in ('kp', 'vp', 'spr', 'spi', 'ks', 'vs', 'ssr', 'ssi')}

    for i in range(DEPTH):
        lam_init = 0.8 - 0.6 * math.exp(-0.3 * i)
        lam_params = jnp.stack([lam_q1[i], lam_k1[i], lam_q2[i], lam_k2[i]]).astype(F32)
        w_in_b = w_in[i].astype(BF16)
        w_out_b = w_out[i].astype(BF16)
        w_glu_b = ssm_w_glu[i].astype(BF16)
        wq_b = peer_w_q[i].astype(BF16)
        keys_b = peer_keys[i].astype(BF16)
        pu_b = peer_u[i].astype(BF16)
        pv_b = peer_v[i].astype(BF16)
        plw_b = ple_w[i].astype(BF16)
        gw_b = ple_gate_w[i].astype(BF16)
        ssm_w = _ssm_weights(ssm_a_re[i], ssm_a_im[i], ssm_b_re[i], ssm_b_im[i], ssm_c_re[i], ssm_c_im[i],
                             ssm_d[i], ssm_log_dt[i])
        last = i == DEPTH - 1

        def tail(h_in, attn, ssm_z, ple):
            ssm_out = _glu(ssm_z, w_glu_b, ssm_b_glu[i])
            h_mid, xn = _out_proj(attn, ssm_out, h_in, w_out_b, norm_ffn_g[i])
            route = _peer_route(xn, wq_b, keys_b)
            peer_out = _peer_dense(xn, pu_b, pv_b, route)
            return _ple(h_mid, peer_out, ple, plw_b, gw_b, final_norm_g, last)

        q, k_new, v_new, u = _in_proj(hp, norm_mix_g[i], w_in_b, BF16)
        attn = _attn_prompt(q, k_new, v_new, bias_prompt, lam_params, subln_g[i], lam_init, bsz, seq, tq)
        z, s_re, s_im = _ssm(_to_chunks(u, bsz, seq), zeros_state, zeros_state, ssm_w,
                             lc=min(128, seq // SSM_CHUNK), bt=8, cdt=BF16, precision=None)
        hp = tail(hp, attn, _from_chunks(z, bsz, seq), p_prompt[i].reshape(bsz * seq, PLE_DIM))
        outs['kp'].append(k_new.reshape(bsz, seq, N_HEADS, VD))
        outs['vp'].append(v_new.reshape(bsz, seq, N_HEADS, VD))
        outs['spr'].append(s_re.reshape(bsz, SSM_GROUPS, SSM_STATE))
        outs['spi'].append(s_im.reshape(bsz, SSM_GROUPS, SSM_STATE))

        q, k_new, v_new, u = _in_proj(hs, norm_mix_g[i], w_in_b, F32)
        attn = _attn_sample(i, page_table, q, k_new, v_new, cache_k, cache_v, bias_pages, bias_new,
                            lam_params, subln_g[i], lam_init, dec_batch, dec_seq)
        z, s_re, s_im = _ssm(_to_chunks(u, dec_batch, dec_seq),
                             state_ssm_re[i].astype(F32).reshape(dec_batch, -1),
                             state_ssm_im[i].astype(F32).reshape(dec_batch, -1), ssm_w,
                             lc=1, bt=dec_batch, cdt=F32, precision=lax.Precision.HIGHEST)
        hs = tail(hs, attn, _from_chunks(z, dec_batch, dec_seq), p_sample[i].reshape(dec_batch * dec_seq, PLE_DIM))
        outs['ks'].append(k_new.reshape(dec_batch, dec_seq, N_HEADS, VD))
        outs['vs'].append(v_new.reshape(dec_batch, dec_seq, N_HEADS, VD))
        outs['ssr'].append(s_re.reshape(dec_batch, SSM_GROUPS, SSM_STATE))
        outs['ssi'].append(s_im.reshape(dec_batch, SSM_GROUPS, SSM_STATE))

    return (hp.reshape(bsz, seq, D_MODEL), hs.reshape(dec_batch, dec_seq, D_MODEL),
            jnp.stack(outs['kp']), jnp.stack(outs['vp']), jnp.stack(outs['spr']), jnp.stack(outs['spi']),
            jnp.stack(outs['ks']), jnp.stack(outs['vs']), jnp.stack(outs['ssr']), jnp.stack(outs['ssi']))
```

```python
import functools
import math

import jax
import jax.numpy as jnp
from jax import lax
from jax.experimental import pallas as pl
from jax.experimental.pallas import tpu as pltpu

F32 = jnp.float32
BF16 = jnp.bfloat16

D_MODEL = 2048
DEPTH = 2
N_HEADS = 8
DK = 64
VD = 128
ATTN_WIDTH = N_HEADS * VD
SSM_WIDTH = 1024
SSM_GROUP = 16
SSM_GROUPS = 64
SSM_STATE = 64
N_BUCKETS = 32
MAX_DISTANCE = 128
PEER_HEADS = 8
PEER_NKEYS = 128
PEER_TOPK = 16
PEER_HALF = 128
PLE_DIM = 256
RMS_EPS = 1e-6
NEG_INF = -1e30

LANES = 128
SSM_CHUNK = 8
SSM_PAIR = 2 * SSM_CHUNK * SSM_GROUP
SSM_PAIRS = SSM_GROUPS // 2
SSM_PAIRS_PER_STEP = 4
SSM_BLOCK = 8
SUBLANES = 8
VMEM_LIMIT = 50 * 1024 * 1024

_PEER_CAND = [(i, j) for i in range(PEER_TOPK) for j in range(PEER_TOPK) if (i + 1) * (j + 1) <= PEER_TOPK]
_PEER_CAND_ROWS = 56


def _params(*sem, flags=None):
    return pltpu.CompilerParams(dimension_semantics=sem, vmem_limit_bytes=VMEM_LIMIT, flags=flags)


def _gelu(x):
    c = math.sqrt(2.0 / math.pi)
    return (0.5 * x) * (1.0 + jnp.tanh(x * (c + (c * 0.044715) * (x * x))))


def _sigmoid(x):
    return 1.0 / (1.0 + jnp.exp(-x))


def _rms(x, g):
    return x * lax.rsqrt(jnp.mean(x * x, axis=-1, keepdims=True) + RMS_EPS) * g


def _in_proj_kernel(x_ref, g_ref, w_ref, q_ref, k_ref, v_ref, u_ref, xn_ref):
    j = pl.program_id(1)

    @pl.when(j == 0)
    def _():
        xn_ref[...] = _rms(x_ref[...], g_ref[...]).astype(BF16)

    acc = jnp.dot(xn_ref[...], w_ref[...], preferred_element_type=F32)

    @pl.when(j == 0)
    def _():
        q_ref[...] = (acc * DK ** -0.5).astype(q_ref.dtype)

    @pl.when(j == 1)
    def _():
        k_ref[...] = acc

    @pl.when(j == 2)
    def _():
        v_ref[...] = acc

    @pl.when(j == 3)
    def _():
        u_ref[...] = acc


def _in_proj(x, g, w_bf16, q_dtype, tm=512):
    m = x.shape[0]
    slab = pl.BlockSpec((tm, ATTN_WIDTH), lambda i, j: (i, 0))
    return pl.pallas_call(
        _in_proj_kernel,
        grid=(m // tm, 4),
        in_specs=[pl.BlockSpec((tm, D_MODEL), lambda i, j: (i, 0)),
                  pl.BlockSpec((1, D_MODEL), lambda i, j: (0, 0)),
                  pl.BlockSpec((D_MODEL, ATTN_WIDTH), lambda i, j: (0, j))],
        out_specs=[slab, slab, slab, slab],
        out_shape=[jax.ShapeDtypeStruct((m, ATTN_WIDTH), q_dtype),
                   jax.ShapeDtypeStruct((m, ATTN_WIDTH), F32),
                   jax.ShapeDtypeStruct((m, ATTN_WIDTH), F32),
                   jax.ShapeDtypeStruct((m, SSM_WIDTH), F32)],
        scratch_shapes=[pltpu.VMEM((tm, D_MODEL), BF16)],
        compiler_params=_params("parallel", "arbitrary"),
        name="in_proj",
    )(x, g.reshape(1, D_MODEL), w_bf16)


def _lam_value(lam_ref, lam_init):
    lp = lam_ref[...]
    l1 = jnp.sum(lp[0:1] * lp[1:2], axis=-1, keepdims=True)
    l2 = jnp.sum(lp[2:3] * lp[3:4], axis=-1, keepdims=True)
    return jnp.exp(l1) - jnp.exp(l2) + lam_init


def _split_maps(q):
    lane = lax.broadcasted_iota(jnp.int32, q.shape, 1)
    zero = jnp.zeros_like(q)
    return jnp.concatenate([jnp.where(lane < DK, q, zero), jnp.where(lane >= DK, q, zero)], axis=0)


def _bias_by_distance(rel_bias, lo, hi):
    rel = jnp.arange(lo, hi)
    n = jnp.maximum(rel, 0)
    max_exact = N_BUCKETS // 2
    nf = jnp.maximum(n, max_exact).astype(F32)
    large = max_exact + (jnp.log(nf / max_exact) / math.log(MAX_DISTANCE / max_exact)
                         * (N_BUCKETS - max_exact)).astype(jnp.int32)
    large = jnp.minimum(large, N_BUCKETS - 1)
    bucket = jnp.where(n < max_exact, n, large)
    onehot = bucket[:, None] == jnp.arange(N_BUCKETS)[None, :]
    bias = jnp.sum(jnp.where(onehot[None], rel_bias.astype(F32).T[:, None, :], 0.0), axis=-1)
    return jnp.where(rel >= 0, bias, NEG_INF)


def _prompt_bias_tiles(rel_bias, nq, tq):
    val = _bias_by_distance(rel_bias, -tq, nq * tq).reshape(N_HEADS, nq + 1, tq)
    v = jnp.concatenate([val[:, 1:], val[:, :-1]], axis=-1)
    x = jnp.broadcast_to(v[:, :, None, :], (N_HEADS, nq, tq, 2 * tq)).reshape(N_HEADS, nq, 2 * tq * tq)
    x = x[:, :, :tq * (2 * tq - 1)].reshape(N_HEADS, nq, tq, 2 * tq - 1)
    return x[:, :, :, :tq]


def _sample_bias_tables(rel_bias, n_pages, dec_seq):
    past_len = n_pages * PAGE_ROWS
    same_head = jnp.eye(N_HEADS, dtype=bool)
    val = _bias_by_distance(rel_bias, 0, past_len + dec_seq)
    bp = jnp.stack([val[:, t + 1:past_len + t + 1][:, ::-1] for t in range(dec_seq)], axis=1)
    bp = jnp.transpose(bp.reshape(N_HEADS, dec_seq, n_pages, PAGE_ROWS), (2, 0, 1, 3))
    bp = jnp.broadcast_to(bp[:, :, None], (n_pages, N_HEADS, 2, dec_seq, PAGE_ROWS))
    pages = jnp.where(same_head[None, :, None, None, None, :], bp[..., None], NEG_INF)
    pages = pages.reshape(n_pages, N_HEADS * 2 * dec_seq, PAGE_ROWS * N_HEADS)
    valn = _bias_by_distance(rel_bias, 1 - dec_seq, dec_seq)
    t_i, s_i = jnp.meshgrid(jnp.arange(dec_seq), jnp.arange(dec_seq), indexing='ij')
    bn = jnp.take(valn, t_i - s_i + dec_seq - 1, axis=1)
    bn = jnp.broadcast_to(bn[:, None], (N_HEADS, 2, dec_seq, dec_seq))
    new = jnp.where(same_head[:, None, None, None, :], bn[..., None], NEG_INF)
    new = new.reshape(N_HEADS * 2 * dec_seq, dec_seq * N_HEADS)
    new = jnp.pad(new, ((0, 0), (0, PAGE_ROWS - dec_seq * N_HEADS)), constant_values=NEG_INF)
    return pages, new


ATTN_HEADS_PER_STEP = 4


def _attn_prompt_kernel(lam_ref, q_ref, k_ref, v_ref, bias_ref, g_ref, o_ref, acc_scr, *, tq, lam_init):
    qi = pl.program_id(2)
    heads = range(ATTN_HEADS_PER_STEP)
    qq = [_split_maps(q_ref[:, h * VD:(h + 1) * VD]) for h in heads]
    acc_scr[...] = jnp.zeros_like(acc_scr)

    def body(kj, carry):
        start = pl.multiple_of(kj * tq, tq)
        ss = []
        for h in heads:
            kb = k_ref[pl.ds(start, tq), h * VD:(h + 1) * VD].astype(BF16)
            ss.append(lax.dot_general(kb, qq[h], (((1,), (1,)), ((), ())), preferred_element_type=F32))
        out, ps, alphas = [], [], []
        for h in heads:
            m_old, l_old = carry[h]
            bias = bias_ref[h, qi - kj]
            s = ss[h] + jnp.concatenate([bias, bias], axis=1)
            m_new = jnp.maximum(m_old, jnp.max(s, axis=0, keepdims=True))
            alpha = jnp.exp(m_old - m_new)
            p = jnp.exp(s - m_new)
            out.append((m_new, alpha * l_old + jnp.sum(p, axis=0, keepdims=True)))
            ps.append(p.astype(BF16))
            alphas.append(alpha)
        for h in heads:
            vb = v_ref[pl.ds(start, tq), h * VD:(h + 1) * VD].astype(BF16)
            pv = lax.dot_general(vb, ps[h], (((0,), (0,)), ((), ())), preferred_element_type=F32)
            acc_scr[h] = alphas[h] * acc_scr[h] + pv
        return tuple(out)

    m0 = jnp.full((1, 2 * tq), NEG_INF, F32)
    l0 = jnp.zeros((1, 2 * tq), F32)
    fin = lax.fori_loop(0, qi + 1, body, tuple((m0, l0) for _ in heads))

    lam = _lam_value(lam_ref, lam_init)
    for h in heads:
        acc = acc_scr[h] / fin[h][1]
        o = acc[:, :tq] - lam * acc[:, tq:]
        o = o * lax.rsqrt(jnp.mean(o * o, axis=0, keepdims=True) + RMS_EPS)
        o = o * g_ref[...] * (1.0 - lam_init)
        o_ref[:, h * VD:(h + 1) * VD] = o.T.astype(o_ref.dtype)


def _attn_prompt(q, k, v, bias_t, lam_params, subln_g, lam_init, bsz, seq, tq=256):
    nq = seq // tq
    hps = ATTN_HEADS_PER_STEP
    kv_spec = pl.BlockSpec((seq, hps * VD), lambda b, h, i: (b, h))
    return pl.pallas_call(
        functools.partial(_attn_prompt_kernel, tq=tq, lam_init=lam_init),
        grid=(bsz, N_HEADS // hps, nq),
        in_specs=[pl.BlockSpec((4, DK), lambda b, h, i: (0, 0)),
                  pl.BlockSpec((tq, hps * VD), lambda b, h, i: (b * nq + i, h)),
                  kv_spec, kv_spec,
                  pl.BlockSpec((hps, nq, tq, tq), lambda b, h, i: (h, 0, 0, 0)),
                  pl.BlockSpec((VD, 1), lambda b, h, i: (0, 0))],
        out_specs=pl.BlockSpec((tq, hps * VD), lambda b, h, i: (b * nq + i, h)),
        out_shape=jax.ShapeDtypeStruct((bsz * seq, ATTN_WIDTH), BF16),
        scratch_shapes=[pltpu.VMEM((hps, VD, 2 * tq), F32)],
        compiler_params=_params("parallel", "parallel", "arbitrary"),
        name="attn_prompt",
    )(lam_params, q, k, v, bias_t, subln_g.reshape(VD, 1))


PAGE_ROWS = 128
SAMPLE_PAGES_PER_STEP = 8


def _attn_sample_kernel(pt_ref, lam_ref, q_ref, kn_ref, vn_ref, *rest, n_steps, pps, dec_seq, lam_init):
    del pt_ref
    kc_refs, vc_refs = rest[:pps], rest[pps:2 * pps]
    bias_ref, biasn_ref, g_ref, o_ref, q_scr, m_scr, l_scr, acc_scr = rest[2 * pps:]
    j = pl.program_id(1)
    nt = (((1,), (1,)), ((), ()))

    @pl.when(j == 0)
    def _():
        parts = [_split_maps(q_ref[:, h * VD:(h + 1) * VD]) for h in range(N_HEADS)]
        q_scr[...] = jnp.concatenate(parts, axis=0).astype(BF16)
        m_scr[...] = jnp.full_like(m_scr, NEG_INF)
        l_scr[...] = jnp.zeros_like(l_scr)
        acc_scr[...] = jnp.zeros_like(acc_scr)

    def update(kbs, vbs, biases):
        q = q_scr[...]
        ss = [lax.dot_general(q, kb, nt, preferred_element_type=F32) + b for kb, b in zip(kbs, biases)]
        m_old = m_scr[...]
        m_new = m_old
        for s in ss:
            m_new = jnp.maximum(m_new, jnp.max(s, axis=-1, keepdims=True))
        alpha = jnp.exp(m_old - m_new)
        l_new = alpha * l_scr[...]
        acc = alpha * acc_scr[...]
        for s, vb in zip(ss, vbs):
            p = jnp.exp(s - m_new)
            l_new = l_new + jnp.sum(p, axis=-1, keepdims=True)
            acc = acc + jnp.dot(p.astype(BF16), vb, preferred_element_type=F32)
        l_scr[...] = l_new
        acc_scr[...] = acc
        m_scr[...] = m_new

    update([r[...].astype(BF16) for r in kc_refs], [r[...].astype(BF16) for r in vc_refs],
           [bias_ref[j * pps + i] for i in range(pps)])

    @pl.when(j == n_steps - 1)
    def _():
        pad = jnp.zeros((PAGE_ROWS - dec_seq * N_HEADS, VD), F32)
        update([jnp.concatenate([kn_ref[...], pad], axis=0).astype(BF16)],
               [jnp.concatenate([vn_ref[...], pad], axis=0).astype(BF16)], [biasn_ref[...]])
        lam = _lam_value(lam_ref, lam_init)
        acc = acc_scr[...] / l_scr[...]
        for h in range(N_HEADS):
            r0 = h * 2 * dec_seq
            o = acc[r0:r0 + dec_seq] - lam * acc[r0 + dec_seq:r0 + 2 * dec_seq]
            o = o * lax.rsqrt(jnp.mean(o * o, axis=-1, keepdims=True) + RMS_EPS)
            o_ref[:, h * VD:(h + 1) * VD] = o * g_ref[...] * (1.0 - lam_init)


def _attn_sample(layer, page_table, q, k_new, v_new, cache_k, cache_v, bias_pages, bias_new,
                 lam_params, subln_g, lam_init, dec_batch, dec_seq):
    n_pages = page_table.shape[1]
    pps = math.gcd(n_pages, SAMPLE_PAGES_PER_STEP)
    rows = PAGE_ROWS * N_HEADS
    qrows = N_HEADS * 2 * dec_seq
    tok_spec = pl.BlockSpec((dec_seq, ATTN_WIDTH), lambda b, j, pt: (b, 0))
    new_spec = pl.BlockSpec((dec_seq * N_HEADS, VD), lambda b, j, pt: (b, 0))
    page_specs = [pl.BlockSpec((None, None, rows, VD), lambda b, j, pt, i=i: (layer, pt[b, j * pps + i], 0, 0))
                  for i in range(pps)]
    grid_spec = pltpu.PrefetchScalarGridSpec(
        num_scalar_prefetch=1,
        grid=(dec_batch, n_pages // pps),
        in_specs=[pl.BlockSpec((4, DK), lambda b, j, pt: (0, 0)),
                  tok_spec, new_spec, new_spec, *page_specs, *page_specs,
                  pl.BlockSpec((n_pages, qrows, rows), lambda b, j, pt: (0, 0, 0)),
                  pl.BlockSpec((qrows, PAGE_ROWS), lambda b, j, pt: (0, 0)),
                  pl.BlockSpec((1, VD), lambda b, j, pt: (0, 0))],
        out_specs=tok_spec,
        scratch_shapes=[pltpu.VMEM((qrows, VD), BF16),
                        pltpu.VMEM((qrows, 1), F32),
                        pltpu.VMEM((qrows, 1), F32),
                        pltpu.VMEM((qrows, VD), F32)])
    return pl.pallas_call(
        functools.partial(_attn_sample_kernel, n_steps=n_pages // pps, pps=pps, dec_seq=dec_seq,
                          lam_init=lam_init),
        grid_spec=grid_spec,
        out_shape=jax.ShapeDtypeStruct((dec_batch * dec_seq, ATTN_WIDTH), F32),
        compiler_params=_params("parallel", "arbitrary"),
        name="attn_sample",
    )(page_table, lam_params, q, k_new.reshape(-1, VD), v_new.reshape(-1, VD),
      *([cache_k] * pps), *([cache_v] * pps), bias_pages, bias_new, subln_g.reshape(1, VD))


def _ssm_weights(a_re, a_im, b_re, b_im, c_re, c_im, d, log_dt):
    hp = lax.Precision.HIGHEST
    g, n, p, c = SSM_GROUPS, SSM_STATE, SSM_GROUP, SSM_CHUNK
    dt = jnp.exp(log_dt.astype(F32))[:, None]
    a_re, a_im = a_re.astype(F32), a_im.astype(F32)
    tau = jnp.arange(c + 1, dtype=F32)[:, None, None]
    mag = jnp.exp(a_re * dt * tau)
    ang = a_im * dt * tau
    pw_re, pw_im = mag * jnp.cos(ang), mag * jnp.sin(ang)
    nr, ni = pw_re[1] - 1.0, pw_im[1]
    den = a_re * a_re + a_im * a_im
    f_re, f_im = (nr * a_re + ni * a_im) / den, (ni * a_re - nr * a_im) / den
    b_re, b_im = b_re.astype(F32), b_im.astype(F32)
    bb_re = f_re[:, :, None] * b_re - f_im[:, :, None] * b_im
    bb_im = f_re[:, :, None] * b_im + f_im[:, :, None] * b_re
    c_re, c_im = c_re.astype(F32), c_im.astype(F32)
    cp_re = c_re[None] * pw_re[:, :, None, :] - c_im[None] * pw_im[:, :, None, :]
    cp_im = c_re[None] * pw_im[:, :, None, :] + c_im[None] * pw_re[:, :, None, :]
    kern = (jnp.einsum('tgpn,gnq->tgpq', cp_re, bb_re, precision=hp)
            - jnp.einsum('tgpn,gnq->tgpq', cp_im, bb_im, precision=hp))
    ss, tt = jnp.meshgrid(jnp.arange(c), jnp.arange(c), indexing='ij')
    lag = tt - ss
    m_intra = jnp.where((lag >= 0)[:, :, None, None, None], kern[jnp.maximum(lag, 0)], 0.0)
    m_intra = jnp.transpose(m_intra, (2, 0, 4, 1, 3)).reshape(g, c * p, c * p)
    rev = pw_re[c - 1::-1][:c], pw_im[c - 1::-1][:c]
    wb_re = rev[0][:, :, :, None] * bb_re[None] - rev[1][:, :, :, None] * bb_im[None]
    wb_im = rev[0][:, :, :, None] * bb_im[None] + rev[1][:, :, :, None] * bb_re[None]
    wb_re = jnp.transpose(wb_re, (1, 0, 3, 2)).reshape(g, c * p, n)
    wb_im = jnp.transpose(wb_im, (1, 0, 3, 2)).reshape(g, c * p, n)
    wc_re = jnp.transpose(cp_re[1:], (1, 3, 0, 2)).reshape(g, n, c * p)
    wc_im = -jnp.transpose(cp_im[1:], (1, 3, 0, 2)).reshape(g, n, c * p)

    def pair_diag(x):
        r, cc = x.shape[1:]
        x = x.reshape(g // 2, 2, r, cc)
        return jnp.einsum('kirc,ij->kirjc', x, jnp.eye(2, dtype=F32), precision=hp).reshape(g // 2, 2 * r, 2 * cc)

    d_chunk = jnp.broadcast_to(d.astype(F32).reshape(g, 1, p), (g, c, p)).reshape(1, g * c * p)

    gb, gl = g // SSM_BLOCK, SSM_BLOCK
    eye = jnp.eye(gl, dtype=F32)
    m_tok = jnp.einsum('bgsqtp,gh->bsgqthp', m_intra.reshape(gb, gl, c, p, c, p), eye,
                       precision=hp).reshape(gb, gl * c * p, gl * c * p)

    def wb_tok(x):
        return jnp.einsum('bgsqn,gh->bsgqhn', x.reshape(gb, gl, c, p, n), eye,
                          precision=hp).reshape(gb, gl * c * p, gl * n)

    def wc_tok(x):
        return jnp.einsum('bgntp,gh->bgnthp', x.reshape(gb, gl, n, c, p), eye,
                          precision=hp).reshape(gb, gl * n, gl * c * p)

    k8 = c * jnp.arange(1, SUBLANES + 1, dtype=F32)[:, None, None]
    mag8, ang8 = jnp.exp(a_re * dt * k8), a_im * dt * k8
    ap_re = jnp.transpose((mag8 * jnp.cos(ang8)).reshape(SUBLANES, gb, gl * n), (1, 0, 2))
    ap_im = jnp.transpose((mag8 * jnp.sin(ang8)).reshape(SUBLANES, gb, gl * n), (1, 0, 2))
    d_tok = jnp.broadcast_to(d.astype(F32).reshape(gb, 1, gl * p), (gb, c, gl * p)).reshape(gb, 1, c * gl * p)
    tok = dict(m=m_tok, wb_re=wb_tok(wb_re), wb_im=wb_tok(wb_im), wc_re=wc_tok(wc_re), wc_im=wc_tok(wc_im),
               ap_re=ap_re, ap_im=ap_im, d=d_tok)
    return dict(m=pair_diag(m_intra), wb_re=pair_diag(wb_re), wb_im=pair_diag(wb_im),
                wc_re=pair_diag(wc_re), wc_im=pair_diag(wc_im),
                a_re=pw_re[c].reshape(1, g * n), a_im=pw_im[c].reshape(1, g * n), d=d_chunk, tok=tok)


def _ssm_tok_kernel(u_ref, m_ref, wbre_ref, wbim_ref, wcre_ref, wcim_ref, apre_ref, apim_ref, d_ref,
                    z_ref, hre_ref, him_ref, gre_scr, gim_scr, *, nc):
    c = SSM_CHUNK
    x = jnp.concatenate([u_ref[pl.ds(s, nc, stride=c), :] for s in range(c)], axis=1)
    xb = x.astype(BF16)
    dot = functools.partial(jnp.dot, preferred_element_type=F32)
    g_re, g_im = dot(xb, wbre_ref[...]), dot(xb, wbim_ref[...])
    ap_re, ap_im = apre_ref[...], apim_ref[...]
    row = lax.broadcasted_iota(jnp.int32, g_re.shape, 0)
    sub = row % SUBLANES
    for k in (1, 2, 4):
        sh_re, sh_im = pltpu.roll(g_re, k, axis=0), pltpu.roll(g_im, k, axis=0)
        a_r, a_i = ap_re[k - 1:k], ap_im[k - 1:k]
        keep = sub >= k
        g_re, g_im = (g_re + jnp.where(keep, a_r * sh_re - a_i * sh_im, 0.0),
                      g_im + jnp.where(keep, a_r * sh_im + a_i * sh_re, 0.0))
    gre_scr[...] = g_re
    gim_scr[...] = g_im

    def tile(i, carry):
        c_re, c_im = carry
        r0 = pl.multiple_of(i * SUBLANES, SUBLANES)
        h_re = gre_scr[pl.ds(r0, SUBLANES), :] + (ap_re * c_re - ap_im * c_im)
        h_im = gim_scr[pl.ds(r0, SUBLANES), :] + (ap_re * c_im + ap_im * c_re)
        gre_scr[pl.ds(r0, SUBLANES), :] = h_re
        gim_scr[pl.ds(r0, SUBLANES), :] = h_im
        return h_re[SUBLANES - 1:], h_im[SUBLANES - 1:]

    zero = jnp.zeros((1, g_re.shape[1]), F32)
    f_re, f_im = lax.fori_loop(0, nc // SUBLANES, tile, (zero, zero))
    hre_ref[...] = f_re
    him_ref[...] = f_im
    e_re = jnp.where(row == 0, 0.0, pltpu.roll(gre_scr[...], 1, axis=0))
    e_im = jnp.where(row == 0, 0.0, pltpu.roll(gim_scr[...], 1, axis=0))
    y = (dot(xb, m_ref[...]) + dot(e_re.astype(BF16), wcre_ref[...]) + dot(e_im.astype(BF16), wcim_ref[...])
         + d_ref[...] * x)
    z = _gelu(y)
    for t in range(c):
        z_ref[pl.ds(t, nc, stride=c), :] = z[:, t * LANES:(t + 1) * LANES]


def _ssm_tok(u, w, bsz, seq):
    nc = seq // SSM_CHUNK
    gb = SSM_GROUPS // SSM_BLOCK
    sl = SSM_BLOCK * SSM_STATE
    wide = SSM_CHUNK * LANES
    tok_spec = pl.BlockSpec((seq, LANES), lambda k, b: (b, k))
    st_spec = pl.BlockSpec((None, 1, sl), lambda k, b: (b, 0, k))

    def wspec(r, cc):
        return pl.BlockSpec((None, r, cc), lambda k, b: (k, 0, 0))

    t = w['tok']
    return pl.pallas_call(
        functools.partial(_ssm_tok_kernel, nc=nc),
        grid=(gb, bsz),
        in_specs=[tok_spec, wspec(wide, wide), wspec(wide, sl), wspec(wide, sl), wspec(sl, wide), wspec(sl, wide),
                  wspec(SUBLANES, sl), wspec(SUBLANES, sl), wspec(1, wide)],
        out_specs=[tok_spec, st_spec, st_spec],
        out_shape=[jax.ShapeDtypeStruct((bsz * seq, SSM_WIDTH), F32),
                   jax.ShapeDtypeStruct((bsz, 1, SSM_GROUPS * SSM_STATE), F32),
                   jax.ShapeDtypeStruct((bsz, 1, SSM_GROUPS * SSM_STATE), F32)],
        scratch_shapes=[pltpu.VMEM((nc, sl), F32), pltpu.VMEM((nc, sl), F32)],
        compiler_params=_params("parallel", "parallel"),
        name="ssm_scan_tok",
    )(u, t['m'].astype(BF16), t['wb_re'].astype(BF16), t['wb_im'].astype(BF16),
      t['wc_re'].astype(BF16), t['wc_im'].astype(BF16), t['ap_re'], t['ap_im'], t['d'])


def _ssm_kernel(u_ref, h0re_ref, h0im_ref, m_ref, wbre_ref, wbim_ref, wcre_ref, wcim_ref,
                are_ref, aim_ref, d_ref, z_ref, hre_ref, him_ref,
                sre_scr, sim_scr, hsre_scr, hsim_scr, *, lc, bt, cdt, precision):
    ci = pl.program_id(2)
    rows = lc * bt
    sw = 2 * SSM_STATE
    dot = functools.partial(jnp.dot, preferred_element_type=F32, precision=precision)

    @pl.when(ci == 0)
    def _():
        hre_ref[...] = h0re_ref[...]
        him_ref[...] = h0im_ref[...]

    for k in range(SSM_PAIRS_PER_STEP):
        ug = u_ref[:, :, k * SSM_PAIR:(k + 1) * SSM_PAIR].reshape(rows, SSM_PAIR).astype(cdt)
        sre_scr[:, k * sw:(k + 1) * sw] = dot(ug, wbre_ref[k])
        sim_scr[:, k * sw:(k + 1) * sw] = dot(ug, wbim_ref[k])

    a_re = are_ref[...]
    a_im = aim_ref[...]

    def step(c, carry):
        h_re, h_im = carry
        r0 = pl.multiple_of(c * bt, bt)
        hsre_scr[pl.ds(r0, bt), :] = h_re
        hsim_scr[pl.ds(r0, bt), :] = h_im
        n_re = a_re * h_re - a_im * h_im + sre_scr[pl.ds(r0, bt), :]
        n_im = a_re * h_im + a_im * h_re + sim_scr[pl.ds(r0, bt), :]
        return n_re, n_im

    h_re, h_im = lax.fori_loop(0, lc, step, (hre_ref[...], him_ref[...]))
    hre_ref[...] = h_re
    him_ref[...] = h_im

    for k in range(SSM_PAIRS_PER_STEP):
        lanes = slice(k * SSM_PAIR, (k + 1) * SSM_PAIR)
        ug = u_ref[:, :, lanes].reshape(rows, SSM_PAIR)
        y = (dot(ug.astype(cdt), m_ref[k])
             + dot(hsre_scr[:, k * sw:(k + 1) * sw].astype(cdt), wcre_ref[k])
             + dot(hsim_scr[:, k * sw:(k + 1) * sw].astype(cdt), wcim_ref[k])
             + d_ref[:, lanes] * ug)
        z_ref[:, :, lanes] = _gelu(y).reshape(lc, bt, SSM_PAIR)


def _ssm(u_chunks, h0_re, h0_im, w, lc, bt, cdt, precision):
    nc, bsz, width = u_chunks.shape
    pps = SSM_PAIRS_PER_STEP
    sw = 2 * SSM_STATE
    u_spec = pl.BlockSpec((lc, bt, pps * SSM_PAIR), lambda p, b, c: (c, b, p))
    st_spec = pl.BlockSpec((bt, pps * sw), lambda p, b, c: (b, p))

    def wspec(r, cc):
        return pl.BlockSpec((pps, r, cc), lambda p, b, c: (p, 0, 0))

    return pl.pallas_call(
        functools.partial(_ssm_kernel, lc=lc, bt=bt, cdt=cdt, precision=precision),
        grid=(SSM_PAIRS // pps, bsz // bt, nc // lc),
        in_specs=[u_spec, st_spec, st_spec,
                  wspec(SSM_PAIR, SSM_PAIR), wspec(SSM_PAIR, sw), wspec(SSM_PAIR, sw),
                  wspec(sw, SSM_PAIR), wspec(sw, SSM_PAIR),
                  pl.BlockSpec((1, pps * sw), lambda p, b, c: (0, p)),
                  pl.BlockSpec((1, pps * sw), lambda p, b, c: (0, p)),
                  pl.BlockSpec((1, pps * SSM_PAIR), lambda p, b, c: (0, p))],
        out_specs=[u_spec, st_spec, st_spec],
        out_shape=[jax.ShapeDtypeStruct((nc, bsz, width), F32),
                   jax.ShapeDtypeStruct(h0_re.shape, F32),
                   jax.ShapeDtypeStruct(h0_im.shape, F32)],
        scratch_shapes=[pltpu.VMEM((lc * bt, pps * sw), F32) for _ in range(4)],
        compiler_params=_params("parallel", "parallel", "arbitrary"),
        name="ssm_scan",
    )(u_chunks, h0_re, h0_im, w['m'].astype(cdt), w['wb_re'].astype(cdt), w['wb_im'].astype(cdt),
      w['wc_re'].astype(cdt), w['wc_im'].astype(cdt), w['a_re'], w['a_im'], w['d'])


def _to_chunks(u, bsz, t):
    nc = t // SSM_CHUNK
    u = u.reshape(bsz, nc, SSM_CHUNK, SSM_GROUPS, SSM_GROUP)
    return jnp.transpose(u, (1, 0, 3, 2, 4)).reshape(nc, bsz, SSM_GROUPS * SSM_CHUNK * SSM_GROUP)


def _from_chunks(z, bsz, t):
    nc = t // SSM_CHUNK
    z = z.reshape(nc, bsz, SSM_GROUPS, SSM_CHUNK, SSM_GROUP)
    return jnp.transpose(z, (1, 0, 3, 2, 4)).reshape(bsz * t, SSM_WIDTH)


def _glu_kernel(z_ref, w_ref, b_ref, o_ref):
    z = z_ref[...]
    gate = jnp.dot(z.astype(BF16), w_ref[...], preferred_element_type=F32) + b_ref[...]
    o_ref[...] = (z * _sigmoid(gate)).astype(o_ref.dtype)


def _glu(z, w_bf16, b, tm=512):
    m = z.shape[0]
    return pl.pallas_call(
        _glu_kernel,
        grid=(m // tm,),
        in_specs=[pl.BlockSpec((tm, SSM_WIDTH), lambda i: (i, 0)),
                  pl.BlockSpec((SSM_WIDTH, SSM_WIDTH), lambda i: (0, 0)),
                  pl.BlockSpec((1, SSM_WIDTH), lambda i: (0, 0))],
        out_specs=pl.BlockSpec((tm, SSM_WIDTH), lambda i: (i, 0)),
        out_shape=jax.ShapeDtypeStruct((m, SSM_WIDTH), BF16),
        compiler_params=_params("parallel"),
        name="ssm_glu",
    )(z, w_bf16, b.reshape(1, SSM_WIDTH))


def _out_proj_kernel(attn_ref, ssm_ref, x_ref, w_ref, g_ref, h_ref, xn_ref):
    mix = (jnp.dot(attn_ref[...].astype(BF16), w_ref[:ATTN_WIDTH, :], preferred_element_type=F32)
           + jnp.dot(ssm_ref[...], w_ref[ATTN_WIDTH:, :], preferred_element_type=F32))
    h = x_ref[...] + mix
    h_ref[...] = h
    xn_ref[...] = _rms(h, g_ref[...]).astype(BF16)


def _out_proj(attn, ssm_out, x, w_bf16, g, tm=256):
    m = x.shape[0]
    return pl.pallas_call(
        _out_proj_kernel,
        grid=(m // tm,),
        in_specs=[pl.BlockSpec((tm, ATTN_WIDTH), lambda i: (i, 0)),
                  pl.BlockSpec((tm, SSM_WIDTH), lambda i: (i, 0)),
                  pl.BlockSpec((tm, D_MODEL), lambda i: (i, 0)),
                  pl.BlockSpec((D_MODEL, D_MODEL), lambda i: (0, 0)),
                  pl.BlockSpec((1, D_MODEL), lambda i: (0, 0))],
        out_specs=[pl.BlockSpec((tm, D_MODEL), lambda i: (i, 0)),
                   pl.BlockSpec((tm, D_MODEL), lambda i: (i, 0))],
        out_shape=[jax.ShapeDtypeStruct((m, D_MODEL), F32),
                   jax.ShapeDtypeStruct((m, D_MODEL), BF16)],
        compiler_params=_params("parallel"),
        name="out_proj",
    )(attn, ssm_out, x, w_bf16, g.reshape(1, D_MODEL))


def _pick16(s, exact):
    iota = lax.broadcasted_iota(jnp.int32, s.shape, 0)
    rank = jnp.full(s.shape, 31, jnp.int32)
    vals = []
    for r in range(PEER_TOPK):
        mx = jnp.max(s, axis=0, keepdims=True)
        hit = s == mx
        if exact:
            hit = iota == jnp.min(jnp.where(hit, iota, s.shape[0]), axis=0, keepdims=True)
        rank = jnp.where(hit, r, rank)
        s = jnp.where(hit, -jnp.inf, s)
        vals.append(mx)
    return vals, rank


def _peer_route_kernel(xn_ref, wq_ref, keys_ref, lena_ref, rkb_ref, ea_ref, eb_ref,
                       q_scr, cand_scr, sel_scr):
    tm = xn_ref.shape[0]
    q_scr[...] = jnp.dot(xn_ref[...], wq_ref[...], preferred_element_type=F32).astype(BF16)
    cand_scr[...] = jnp.full_like(cand_scr, -jnp.inf)
    nt = (((1,), (1,)), ((), ()))

    def head(idx, _):
        h = idx // (tm // LANES)
        sub = idx % (tm // LANES)
        r0 = pl.multiple_of(sub * LANES, LANES)
        c0 = pl.multiple_of(h * 2 * PEER_HALF, 2 * PEER_HALF)
        qa = q_scr[pl.ds(r0, LANES), pl.ds(c0, PEER_HALF)]
        qb = q_scr[pl.ds(r0, LANES), pl.ds(c0 + PEER_HALF, PEER_HALF)]
        s_a = lax.dot_general(keys_ref[0], qa, nt, preferred_element_type=F32)
        s_b = lax.dot_general(keys_ref[1], qb, nt, preferred_element_type=F32)

        def route(exact):
            va, ra = _pick16(s_a, exact)
            vb, rb = _pick16(s_b, exact)
            for p, (i, j) in enumerate(_PEER_CAND):
                cand_scr[p:p + 1, :] = va[i] + vb[j]
            cand = cand_scr[...]
            _, rc = _pick16(cand, exact)
            sel = jnp.where(rc < 31, 1.0, 0.0)
            top = va[0] + vb[0]
            z = jnp.sum(sel * jnp.exp(cand - top), axis=0, keepdims=True)
            sel_scr[...] = sel
            lena = jnp.zeros(s_a.shape, F32)
            p = 0
            for i in range(PEER_TOPK):
                n_i = PEER_TOPK // (i + 1)
                len_i = jnp.sum(sel_scr[p:p + n_i, :], axis=0, keepdims=True)
                lena = jnp.where(ra == i, len_i, lena)
                p += n_i
            lena_ref[h, :, pl.ds(r0, LANES)] = lena
            rkb_ref[h, :, pl.ds(r0, LANES)] = rb.astype(F32).astype(BF16)
            ea_ref[h, :, pl.ds(r0, LANES)] = jnp.exp(s_a - va[0]) * (1.0 / z)
            eb_ref[h, :, pl.ds(r0, LANES)] = jnp.exp(s_b - vb[0]).astype(BF16)
            ranked = (jnp.sum(jnp.where(ra < 31, 1.0, 0.0), axis=0, keepdims=True)
                      + jnp.sum(jnp.where(rb < 31, 1.0, 0.0), axis=0, keepdims=True)
                      + jnp.sum(sel, axis=0, keepdims=True))
            return jnp.max(ranked)

        ranked = route(exact=False)

        @pl.when(ranked > 3 * PEER_TOPK)
        def _():
            route(exact=True)

        return 0

    lax.fori_loop(0, PEER_HEADS * (tm // LANES), head, 0)


def _peer_route(xn, wq_bf16, keys_bf16, tm=512):
    m = xn.shape[0]
    out_spec = pl.BlockSpec((PEER_HEADS, PEER_NKEYS, tm), lambda i: (0, 0, i))
    out_sds = [jax.ShapeDtypeStruct((PEER_HEADS, PEER_NKEYS, m), dt) for dt in (F32, BF16, F32, BF16)]
    return pl.pallas_call(
        _peer_route_kernel,
        grid=(m // tm,),
        in_specs=[pl.BlockSpec((tm, D_MODEL), lambda i: (i, 0)),
                  pl.BlockSpec((D_MODEL, D_MODEL), lambda i: (0, 0)),
                  pl.BlockSpec((2, PEER_NKEYS, PEER_HALF), lambda i: (0, 0, 0))],
        out_specs=[out_spec] * 4,
        out_shape=out_sds,
        scratch_shapes=[pltpu.VMEM((tm, D_MODEL), BF16),
                        pltpu.VMEM((_PEER_CAND_ROWS, LANES), F32),
                        pltpu.VMEM((_PEER_CAND_ROWS, LANES), F32)],
        compiler_params=_params("parallel"),
        name="peer_route",
    )(xn, wq_bf16, keys_bf16)


def _peer_dense_kernel(xn_ref, u_ref, v_ref, lena_ref, rkb_ref, ea_ref, eb_ref, o_ref,
                       pre0, pre1, ht0, ht1, *, ta, nj, n_tiles):
    s = pl.program_id(0)
    pair_c = s - 2

    @pl.when(s == 0)
    def _():
        for ref in (pre0, pre1, ht0, ht1):
            ref[...] = jnp.zeros_like(ref)

    @pl.when((pair_c >= 0) & (pair_c % nj == 0))
    def _():
        o_ref[...] = jnp.zeros_like(o_ref)

    def step(pre_w, pre_r, ht_w, ht_r):
        @pl.when(pair_c >= 0)
        def _():
            o_ref[...] += lax.dot_general(ht_r[...], v_ref[...], (((0,), (0,)), ((), ())),
                                          preferred_element_type=F32)

        tile = jnp.clip(s - 1, 0, n_tiles - 1) % nj
        for t in range(ta):
            a = tile * ta + t
            rows = slice(t * PEER_NKEYS, (t + 1) * PEER_NKEYS)
            w = None
            for h in range(PEER_HEADS):
                lena = lena_ref[h, pl.ds(a, 1), :].astype(BF16)
                ea = ea_ref[h, pl.ds(a, 1), :].astype(BF16)
                term = jnp.where(rkb_ref[h] < lena, eb_ref[h], jnp.zeros((), BF16)) * ea
                w = term if w is None else w + term
            ht_w[rows, :] = w * _gelu(pre_r[rows, :]).astype(BF16)
        pre_w[...] = lax.dot_general(u_ref[...], xn_ref[...], (((1,), (1,)), ((), ())),
                                     preferred_element_type=F32)

    @pl.when(s % 2 == 0)
    def _():
        step(pre0, pre1, ht1, ht0)

    @pl.when(s % 2 == 1)
    def _():
        step(pre1, pre0, ht0, ht1)


def _peer_dense(xn, u_bf16, v_bf16, route, tm=512, ta=4):
    m = xn.shape[0]
    nj = PEER_NKEYS // ta
    n_tiles = (m // tm) * nj
    rows = ta * PEER_NKEYS

    def pair(s, lag):
        return jnp.clip(s - lag, 0, n_tiles - 1)

    r_spec = pl.BlockSpec((PEER_HEADS, PEER_NKEYS, tm), lambda s: (0, 0, pair(s, 1) // nj))
    return pl.pallas_call(
        functools.partial(_peer_dense_kernel, ta=ta, nj=nj, n_tiles=n_tiles),
        grid=(n_tiles + 2,),
        in_specs=[pl.BlockSpec((tm, D_MODEL), lambda s: (pair(s, 0) // nj, 0)),
                  pl.BlockSpec((rows, D_MODEL), lambda s: (pair(s, 0) % nj, 0)),
                  pl.BlockSpec((rows, D_MODEL), lambda s: (pair(s, 2) % nj, 0)),
                  r_spec, r_spec, r_spec, r_spec],
        out_specs=pl.BlockSpec((tm, D_MODEL), lambda s: (pair(s, 2) // nj, 0)),
        out_shape=jax.ShapeDtypeStruct((m, D_MODEL), F32),
        scratch_shapes=[pltpu.VMEM((rows, tm), F32), pltpu.VMEM((rows, tm), F32),
                        pltpu.VMEM((rows, tm), BF16), pltpu.VMEM((rows, tm), BF16)],
        compiler_params=_params("arbitrary"),
        name="peer_dense",
    )(xn, u_bf16, v_bf16, *route)


def _ple_kernel(h_ref, peer_ref, ple_ref, pw_ref, gw_ref, g_ref, o_ref, *, final_norm):
    h = h_ref[...] + peer_ref[...]
    gate = _sigmoid(jnp.dot(h.astype(BF16), gw_ref[...], preferred_element_type=F32))
    emb = jnp.dot(ple_ref[...].astype(BF16), pw_ref[...], preferred_element_type=F32)
    h = h + emb * gate
    if final_norm:
        h = _rms(h, g_ref[...])
    o_ref[...] = h


def _ple(h, peer_out, ple, pw_bf16, gw_bf16, g, final_norm, tm=256):
    m = h.shape[0]
    row = pl.BlockSpec((tm, D_MODEL), lambda i: (i, 0))
    return pl.pallas_call(
        functools.partial(_ple_kernel, final_norm=final_norm),
        grid=(m // tm,),
        in_specs=[row, row,
                  pl.BlockSpec((tm, PLE_DIM), lambda i: (i, 0)),
                  pl.BlockSpec((PLE_DIM, D_MODEL), lambda i: (0, 0)),
                  pl.BlockSpec((D_MODEL, D_MODEL), lambda i: (0, 0)),
                  pl.BlockSpec((1, D_MODEL), lambda i: (0, 0))],
        out_specs=row,
        out_shape=jax.ShapeDtypeStruct((m, D_MODEL), F32),
        compiler_params=_params("parallel"),
        name="ple_gate",
    )(h, peer_out, ple, pw_bf16, gw_bf16, g.reshape(1, D_MODEL))


def kernel(x_prompt, x_sample, cache_k, cache_v, state_ssm_re, state_ssm_im, page_table, p_prompt, p_sample, norm_mix_g, norm_ffn_g, w_in, w_out, lam_q1, lam_k1, lam_q2, lam_k2, subln_g, rel_bias, ssm_a_re, ssm_a_im, ssm_b_re, ssm_b_im, ssm_c_re, ssm_c_im, ssm_d, ssm_log_dt, ssm_w_glu, ssm_b_glu, peer_w_q, peer_keys, peer_u, peer_v, ple_w, ple_gate_w, final_norm_g):
    bsz, seq = x_prompt.shape[:2]
    dec_batch, dec_seq = x_sample.shape[:2]
    n_pool = cache_k.shape[1]
    n_pages = page_table.shape[1]
    tq = 256
    nq = seq // tq

    bias_prompt = _prompt_bias_tiles(rel_bias, nq, tq)
    bias_pages, bias_new = _sample_bias_tables(rel_bias, n_pages, dec_seq)

    cache_k = cache_k.reshape(DEPTH, n_pool, PAGE_ROWS * N_HEADS, VD)
    cache_v = cache_v.reshape(DEPTH, n_pool, PAGE_ROWS * N_HEADS, VD)
    zeros_state = jnp.zeros((bsz, SSM_GROUPS * SSM_STATE), F32)

    hp = x_prompt.reshape(bsz * seq, D_MODEL)
    hs = x_sample.reshape(dec_batch * dec_seq, D_MODEL)
    outs = {k: [] for k in ('kp', 'vp', 'spr', 'spi', 'ks', 'vs', 'ssr', 'ssi')}

    for i in range(DEPTH):
        lam_init = 0.8 - 0.6 * math.exp(-0.3 * i)
        lam_params = jnp.stack([lam_q1[i], lam_k1[i], lam_q2[i], lam_k2[i]]).astype(F32)
        w_in_b = w_in[i].astype(BF16)
        w_out_b = w_out[i].astype(BF16)
        w_glu_b = ssm_w_glu[i].astype(BF16)
        wq_b = peer_w_q[i].astype(BF16)
        keys_b = peer_keys[i].astype(BF16)
        pu_b = peer_u[i].astype(BF16)
        pv_b = peer_v[i].astype(BF16)
        plw_b = ple_w[i].astype(BF16)
        gw_b = ple_gate_w[i].astype(BF16)
        ssm_w = _ssm_weights(ssm_a_re[i], ssm_a_im[i], ssm_b_re[i], ssm_b_im[i], ssm_c_re[i], ssm_c_im[i],
                             ssm_d[i], ssm_log_dt[i])
        last = i == DEPTH - 1

        def tail(h_in, attn, ssm_z, ple):
            ssm_out = _glu(ssm_z, w_glu_b, ssm_b_glu[i])
            h_mid, xn = _out_proj(attn, ssm_out, h_in, w_out_b, norm_ffn_g[i])
            route = _peer_route(xn, wq_b, keys_b)
            peer_out = _peer_dense(xn, pu_b, pv_b, route)
            return _ple(h_mid, peer_out, ple, plw_b, gw_b, final_norm_g, last)

        q, k_new, v_new, u = _in_proj(hp, norm_mix_g[i], w_in_b, BF16)
        attn = _attn_prompt(q, k_new, v_new, bias_prompt, lam_params, subln_g[i], lam_init, bsz, seq, tq)
        z, s_re, s_im = _ssm_tok(u, ssm_w, bsz, seq)
        hp = tail(hp, attn, z, p_prompt[i].reshape(bsz * seq, PLE_DIM))
        outs['kp'].append(k_new.reshape(bsz, seq, N_HEADS, VD))
        outs['vp'].append(v_new.reshape(bsz, seq, N_HEADS, VD))
        outs['spr'].append(s_re.reshape(bsz, SSM_GROUPS, SSM_STATE))
        outs['spi'].append(s_im.reshape(bsz, SSM_GROUPS, SSM_STATE))

        q, k_new, v_new, u = _in_proj(hs, norm_mix_g[i], w_in_b, F32)
        attn = _attn_sample(i, page_table, q, k_new, v_new, cache_k, cache_v, bias_pages, bias_new,
                            lam_params, subln_g[i], lam_init, dec_batch, dec_seq)
        z, s_re, s_im = _ssm(_to_chunks(u, dec_batch, dec_seq),
                             state_ssm_re[i].astype(F32).reshape(dec_batch, -1),
                             state_ssm_im[i].astype(F32).reshape(dec_batch, -1), ssm_w,
                             lc=1, bt=dec_batch, cdt=F32, precision=lax.Precision.HIGHEST)
        hs = tail(hs, attn, _from_chunks(z, dec_batch, dec_seq), p_sample[i].reshape(dec_batch * dec_seq, PLE_DIM))
        outs['ks'].append(k_new.reshape(dec_batch, dec_seq, N_HEADS, VD))
        outs['vs'].append(v_new.reshape(dec_batch, dec_seq, N_HEADS, VD))
        outs['ssr'].append(s_re.reshape(dec_batch, SSM_GROUPS, SSM_STATE))
        outs['ssi'].append(s_im.reshape(dec_batch, SSM_GROUPS, SSM_STATE))

    return (hp.reshape(bsz, seq, D_MODEL), hs.reshape(dec_batch, dec_seq, D_MODEL),
            jnp.stack(outs['kp']), jnp.stack(outs['vp']), jnp.stack(outs['spr']), jnp.stack(outs['spi']),
            jnp.stack(outs['ks']), jnp.stack(outs['vs']), jnp.stack(outs['ssr']), jnp.stack(outs['ssi']))
```

```python
import functools
import math

import jax
import jax.numpy as jnp
from jax import lax
from jax.experimental import pallas as pl
from jax.experimental.pallas import tpu as pltpu

F32 = jnp.float32
BF16 = jnp.bfloat16

D_MODEL = 2048
DEPTH = 2
N_HEADS = 8
DK = 64
VD = 128
ATTN_WIDTH = N_HEADS * VD
SSM_WIDTH = 1024
SSM_GROUP = 16
SSM_GROUPS = 64
SSM_STATE = 64
N_BUCKETS = 32
MAX_DISTANCE = 128
PEER_HEADS = 8
PEER_NKEYS = 128
PEER_TOPK = 16
PEER_HALF = 128
PLE_DIM = 256
RMS_EPS = 1e-6
NEG_INF = -1e30

LANES = 128
SSM_CHUNK = 8
SSM_PAIR = 2 * SSM_CHUNK * SSM_GROUP
SSM_PAIRS = SSM_GROUPS // 2
SSM_PAIRS_PER_STEP = 4
SSM_BLOCK = 8
SUBLANES = 8
VMEM_LIMIT = 50 * 1024 * 1024

_PEER_CAND = [(i, j) for i in range(PEER_TOPK) for j in range(PEER_TOPK) if (i + 1) * (j + 1) <= PEER_TOPK]
_PEER_CAND_ROWS = 56


def _params(*sem, flags=None):
    return pltpu.CompilerParams(dimension_semantics=sem, vmem_limit_bytes=VMEM_LIMIT, flags=flags)


def _gelu(x):
    c = math.sqrt(2.0 / math.pi)
    return (0.5 * x) * (1.0 + jnp.tanh(x * (c + (c * 0.044715) * (x * x))))


def _sigmoid(x):
    return 1.0 / (1.0 + jnp.exp(-x))


def _rms(x, g):
    return x * lax.rsqrt(jnp.mean(x * x, axis=-1, keepdims=True) + RMS_EPS) * g


def _in_proj_kernel(x_ref, g_ref, w_ref, q_ref, k_ref, v_ref, u_ref, xn_ref):
    j = pl.program_id(1)

    @pl.when(j == 0)
    def _():
        xn_ref[...] = _rms(x_ref[...], g_ref[...]).astype(BF16)

    acc = jnp.dot(xn_ref[...], w_ref[...], preferred_element_type=F32)

    @pl.when(j == 0)
    def _():
        q_ref[...] = (acc * DK ** -0.5).astype(q_ref.dtype)

    @pl.when(j == 1)
    def _():
        k_ref[...] = acc

    @pl.when(j == 2)
    def _():
        v_ref[...] = acc

    @pl.when(j == 3)
    def _():
        u_ref[...] = acc


def _in_proj(x, g, w_bf16, q_dtype, tm=512):
    m = x.shape[0]
    slab = pl.BlockSpec((tm, ATTN_WIDTH), lambda i, j: (i, 0))
    return pl.pallas_call(
        _in_proj_kernel,
        grid=(m // tm, 4),
        in_specs=[pl.BlockSpec((tm, D_MODEL), lambda i, j: (i, 0)),
                  pl.BlockSpec((1, D_MODEL), lambda i, j: (0, 0)),
                  pl.BlockSpec((D_MODEL, ATTN_WIDTH), lambda i, j: (0, j))],
        out_specs=[slab, slab, slab, slab],
        out_shape=[jax.ShapeDtypeStruct((m, ATTN_WIDTH), q_dtype),
                   jax.ShapeDtypeStruct((m, ATTN_WIDTH), F32),
                   jax.ShapeDtypeStruct((m, ATTN_WIDTH), F32),
                   jax.ShapeDtypeStruct((m, SSM_WIDTH), F32)],
        scratch_shapes=[pltpu.VMEM((tm, D_MODEL), BF16)],
        compiler_params=_params("parallel", "arbitrary"),
        name="in_proj",
    )(x, g.reshape(1, D_MODEL), w_bf16)


def _lam_value(lam_ref, lam_init):
    lp = lam_ref[...]
    l1 = jnp.sum(lp[0:1] * lp[1:2], axis=-1, keepdims=True)
    l2 = jnp.sum(lp[2:3] * lp[3:4], axis=-1, keepdims=True)
    return jnp.exp(l1) - jnp.exp(l2) + lam_init


def _split_maps(q):
    lane = lax.broadcasted_iota(jnp.int32, q.shape, 1)
    zero = jnp.zeros_like(q)
    return jnp.concatenate([jnp.where(lane < DK, q, zero), jnp.where(lane >= DK, q, zero)], axis=0)


def _bias_by_distance(rel_bias, lo, hi):
    rel = jnp.arange(lo, hi)
    n = jnp.maximum(rel, 0)
    max_exact = N_BUCKETS // 2
    nf = jnp.maximum(n, max_exact).astype(F32)
    large = max_exact + (jnp.log(nf / max_exact) / math.log(MAX_DISTANCE / max_exact)
                         * (N_BUCKETS - max_exact)).astype(jnp.int32)
    large = jnp.minimum(large, N_BUCKETS - 1)
    bucket = jnp.where(n < max_exact, n, large)
    onehot = bucket[:, None] == jnp.arange(N_BUCKETS)[None, :]
    bias = jnp.sum(jnp.where(onehot[None], rel_bias.astype(F32).T[:, None, :], 0.0), axis=-1)
    return jnp.where(rel >= 0, bias, NEG_INF)


def _prompt_bias_tiles(rel_bias, nq, tq):
    val = _bias_by_distance(rel_bias, -tq, nq * tq).reshape(N_HEADS, nq + 1, tq)
    v = jnp.concatenate([val[:, 1:], val[:, :-1]], axis=-1)
    x = jnp.broadcast_to(v[:, :, None, :], (N_HEADS, nq, tq, 2 * tq)).reshape(N_HEADS, nq, 2 * tq * tq)
    x = x[:, :, :tq * (2 * tq - 1)].reshape(N_HEADS, nq, tq, 2 * tq - 1)
    return x[:, :, :, :tq]


def _sample_bias_tables(rel_bias, n_pages, dec_seq):
    past_len = n_pages * PAGE_ROWS
    same_head = jnp.eye(N_HEADS, dtype=bool)
    val = _bias_by_distance(rel_bias, 0, past_len + dec_seq)
    bp = jnp.stack([val[:, t + 1:past_len + t + 1][:, ::-1] for t in range(dec_seq)], axis=1)
    bp = jnp.transpose(bp.reshape(N_HEADS, dec_seq, n_pages, PAGE_ROWS), (2, 0, 1, 3))
    bp = jnp.broadcast_to(bp[:, :, None], (n_pages, N_HEADS, 2, dec_seq, PAGE_ROWS))
    pages = jnp.where(same_head[None, :, None, None, None, :], bp[..., None], NEG_INF)
    pages = pages.reshape(n_pages, N_HEADS * 2 * dec_seq, PAGE_ROWS * N_HEADS)
    valn = _bias_by_distance(rel_bias, 1 - dec_seq, dec_seq)
    t_i, s_i = jnp.meshgrid(jnp.arange(dec_seq), jnp.arange(dec_seq), indexing='ij')
    bn = jnp.take(valn, t_i - s_i + dec_seq - 1, axis=1)
    bn = jnp.broadcast_to(bn[:, None], (N_HEADS, 2, dec_seq, dec_seq))
    new = jnp.where(same_head[:, None, None, None, :], bn[..., None], NEG_INF)
    new = new.reshape(N_HEADS * 2 * dec_seq, dec_seq * N_HEADS)
    new = jnp.pad(new, ((0, 0), (0, PAGE_ROWS - dec_seq * N_HEADS)), constant_values=NEG_INF)
    return pages, new


ATTN_HEADS_PER_STEP = 4


def _attn_prompt_kernel(lam_ref, q_ref, k_ref, v_ref, bias_ref, g_ref, o_ref, acc_scr, *, tq, lam_init):
    qi = pl.program_id(2)
    heads = range(ATTN_HEADS_PER_STEP)
    qq = [_split_maps(q_ref[:, h * VD:(h + 1) * VD]) for h in heads]
    acc_scr[...] = jnp.zeros_like(acc_scr)

    def body(kj, carry):
        start = pl.multiple_of(kj * tq, tq)
        ss = []
        for h in heads:
            kb = k_ref[pl.ds(start, tq), h * VD:(h + 1) * VD].astype(BF16)
            ss.append(lax.dot_general(kb, qq[h], (((1,), (1,)), ((), ())), preferred_element_type=F32))
        out, ps, alphas = [], [], []
        for h in heads:
            m_old, l_old = carry[h]
            bias = bias_ref[h, qi - kj]
            s = ss[h] + jnp.concatenate([bias, bias], axis=1)
            m_new = jnp.maximum(m_old, jnp.max(s, axis=0, keepdims=True))
            alpha = jnp.exp(m_old - m_new)
            p = jnp.exp(s - m_new)
            out.append((m_new, alpha * l_old + jnp.sum(p, axis=0, keepdims=True)))
            ps.append(p.astype(BF16))
            alphas.append(alpha)
        for h in heads:
            vb = v_ref[pl.ds(start, tq), h * VD:(h + 1) * VD].astype(BF16)
            pv = lax.dot_general(vb, ps[h], (((0,), (0,)), ((), ())), preferred_element_type=F32)
            acc_scr[h] = alphas[h] * acc_scr[h] + pv
        return tuple(out)

    m0 = jnp.full((1, 2 * tq), NEG_INF, F32)
    l0 = jnp.zeros((1, 2 * tq), F32)
    fin = lax.fori_loop(0, qi + 1, body, tuple((m0, l0) for _ in heads))

    lam = _lam_value(lam_ref, lam_init)
    for h in heads:
        acc = acc_scr[h] / fin[h][1]
        o = acc[:, :tq] - lam * acc[:, tq:]
        o = o * lax.rsqrt(jnp.mean(o * o, axis=0, keepdims=True) + RMS_EPS)
        o = o * g_ref[...] * (1.0 - lam_init)
        o_ref[:, h * VD:(h + 1) * VD] = o.T.astype(o_ref.dtype)


def _attn_prompt(q, k, v, bias_t, lam_params, subln_g, lam_init, bsz, seq, tq=256):
    nq = seq // tq
    hps = ATTN_HEADS_PER_STEP
    kv_spec = pl.BlockSpec((seq, hps * VD), lambda b, h, i: (b, h))
    return pl.pallas_call(
        functools.partial(_attn_prompt_kernel, tq=tq, lam_init=lam_init),
        grid=(bsz, N_HEADS // hps, nq),
        in_specs=[pl.BlockSpec((4, DK), lambda b, h, i: (0, 0)),
                  pl.BlockSpec((tq, hps * VD), lambda b, h, i: (b * nq + i, h)),
                  kv_spec, kv_spec,
                  pl.BlockSpec((hps, nq, tq, tq), lambda b, h, i: (h, 0, 0, 0)),
                  pl.BlockSpec((VD, 1), lambda b, h, i: (0, 0))],
        out_specs=pl.BlockSpec((tq, hps * VD), lambda b, h, i: (b * nq + i, h)),
        out_shape=jax.ShapeDtypeStruct((bsz * seq, ATTN_WIDTH), BF16),
        scratch_shapes=[pltpu.VMEM((hps, VD, 2 * tq), F32)],
        compiler_params=_params("parallel", "parallel", "arbitrary"),
        name="attn_prompt",
    )(lam_params, q, k, v, bias_t, subln_g.reshape(VD, 1))


PAGE_ROWS = 128
SAMPLE_PAGES_PER_STEP = 8


def _attn_sample_kernel(pt_ref, lam_ref, q_ref, kn_ref, vn_ref, *rest, n_steps, pps, dec_seq, lam_init):
    del pt_ref
    kc_refs, vc_refs = rest[:pps], rest[pps:2 * pps]
    bias_ref, biasn_ref, g_ref, o_ref, q_scr, m_scr, l_scr, acc_scr = rest[2 * pps:]
    j = pl.program_id(1)
    nt = (((1,), (1,)), ((), ()))

    @pl.when(j == 0)
    def _():
        parts = [_split_maps(q_ref[:, h * VD:(h + 1) * VD]) for h in range(N_HEADS)]
        q_scr[...] = jnp.concatenate(parts, axis=0).astype(BF16)
        m_scr[...] = jnp.full_like(m_scr, NEG_INF)
        l_scr[...] = jnp.zeros_like(l_scr)
        acc_scr[...] = jnp.zeros_like(acc_scr)

    def update(kbs, vbs, biases):
        q = q_scr[...]
        ss = [lax.dot_general(q, kb, nt, preferred_element_type=F32) + b for kb, b in zip(kbs, biases)]
        m_old = m_scr[...]
        m_new = m_old
        for s in ss:
            m_new = jnp.maximum(m_new, jnp.max(s, axis=-1, keepdims=True))
        alpha = jnp.exp(m_old - m_new)
        l_new = alpha * l_scr[...]
        acc = alpha * acc_scr[...]
        for s, vb in zip(ss, vbs):
            p = jnp.exp(s - m_new)
            l_new = l_new + jnp.sum(p, axis=-1, keepdims=True)
            acc = acc + jnp.dot(p.astype(BF16), vb, preferred_element_type=F32)
        l_scr[...] = l_new
        acc_scr[...] = acc
        m_scr[...] = m_new

    update([r[...].astype(BF16) for r in kc_refs], [r[...].astype(BF16) for r in vc_refs],
           [bias_ref[j * pps + i] for i in range(pps)])

    @pl.when(j == n_steps - 1)
    def _():
        pad = jnp.zeros((PAGE_ROWS - dec_seq * N_HEADS, VD), F32)
        update([jnp.concatenate([kn_ref[...], pad], axis=0).astype(BF16)],
               [jnp.concatenate([vn_ref[...], pad], axis=0).astype(BF16)], [biasn_ref[...]])
        lam = _lam_value(lam_ref, lam_init)
        acc = acc_scr[...] / l_scr[...]
        for h in range(N_HEADS):
            r0 = h * 2 * dec_seq
            o = acc[r0:r0 + dec_seq] - lam * acc[r0 + dec_seq:r0 + 2 * dec_seq]
            o = o * lax.rsqrt(jnp.mean(o * o, axis=-1, keepdims=True) + RMS_EPS)
            o_ref[:, h * VD:(h + 1) * VD] = o * g_ref[...] * (1.0 - lam_init)


def _attn_sample(layer, page_table, q, k_new, v_new, cache_k, cache_v, bias_pages, bias_new,
                 lam_params, subln_g, lam_init, dec_batch, dec_seq):
    n_pages = page_table.shape[1]
    pps = math.gcd(n_pages, SAMPLE_PAGES_PER_STEP)
    rows = PAGE_ROWS * N_HEADS
    qrows = N_HEADS * 2 * dec_seq
    tok_spec = pl.BlockSpec((dec_seq, ATTN_WIDTH), lambda b, j, pt: (b, 0))
    new_spec = pl.BlockSpec((dec_seq * N_HEADS, VD), lambda b, j, pt: (b, 0))
    page_specs = [pl.BlockSpec((None, None, rows, VD), lambda b, j, pt, i=i: (layer, pt[b, j * pps + i], 0, 0))
                  for i in range(pps)]
    grid_spec = pltpu.PrefetchScalarGridSpec(
        num_scalar_prefetch=1,
        grid=(dec_batch, n_pages // pps),
        in_specs=[pl.BlockSpec((4, DK), lambda b, j, pt: (0, 0)),
                  tok_spec, new_spec, new_spec, *page_specs, *page_specs,
                  pl.BlockSpec((n_pages, qrows, rows), lambda b, j, pt: (0, 0, 0)),
                  pl.BlockSpec((qrows, PAGE_ROWS), lambda b, j, pt: (0, 0)),
                  pl.BlockSpec((1, VD), lambda b, j, pt: (0, 0))],
        out_specs=tok_spec,
        scratch_shapes=[pltpu.VMEM((qrows, VD), BF16),
                        pltpu.VMEM((qrows, 1), F32),
                        pltpu.VMEM((qrows, 1), F32),
                        pltpu.VMEM((qrows, VD), F32)])
    return pl.pallas_call(
        functools.partial(_attn_sample_kernel, n_steps=n_pages // pps, pps=pps, dec_seq=dec_seq,
                          lam_init=lam_init),
        grid_spec=grid_spec,
        out_shape=jax.ShapeDtypeStruct((dec_batch * dec_seq, ATTN_WIDTH), F32),
        compiler_params=_params("parallel", "arbitrary"),
        name="attn_sample",
    )(page_table, lam_params, q, k_new.reshape(-1, VD), v_new.reshape(-1, VD),
      *([cache_k] * pps), *([cache_v] * pps), bias_pages, bias_new, subln_g.reshape(1, VD))


def _ssm_weights(a_re, a_im, b_re, b_im, c_re, c_im, d, log_dt):
    hp = lax.Precision.HIGHEST
    g, n, p, c = SSM_GROUPS, SSM_STATE, SSM_GROUP, SSM_CHUNK
    dt = jnp.exp(log_dt.astype(F32))[:, None]
    a_re, a_im = a_re.astype(F32), a_im.astype(F32)
    tau = jnp.arange(c + 1, dtype=F32)[:, None, None]
    mag = jnp.exp(a_re * dt * tau)
    ang = a_im * dt * tau
    pw_re, pw_im = mag * jnp.cos(ang), mag * jnp.sin(ang)
    nr, ni = pw_re[1] - 1.0, pw_im[1]
    den = a_re * a_re + a_im * a_im
    f_re, f_im = (nr * a_re + ni * a_im) / den, (ni * a_re - nr * a_im) / den
    b_re, b_im = b_re.astype(F32), b_im.astype(F32)
    bb_re = f_re[:, :, None] * b_re - f_im[:, :, None] * b_im
    bb_im = f_re[:, :, None] * b_im + f_im[:, :, None] * b_re
    c_re, c_im = c_re.astype(F32), c_im.astype(F32)
    cp_re = c_re[None] * pw_re[:, :, None, :] - c_im[None] * pw_im[:, :, None, :]
    cp_im = c_re[None] * pw_im[:, :, None, :] + c_im[None] * pw_re[:, :, None, :]
    kern = (jnp.einsum('tgpn,gnq->tgpq', cp_re, bb_re, precision=hp)
            - jnp.einsum('tgpn,gnq->tgpq', cp_im, bb_im, precision=hp))
    ss, tt = jnp.meshgrid(jnp.arange(c), jnp.arange(c), indexing='ij')
    lag = tt - ss
    m_intra = jnp.where((lag >= 0)[:, :, None, None, None], kern[jnp.maximum(lag, 0)], 0.0)
    m_intra = jnp.transpose(m_intra, (2, 0, 4, 1, 3)).reshape(g, c * p, c * p)
    rev = pw_re[c - 1::-1][:c], pw_im[c - 1::-1][:c]
    wb_re = rev[0][:, :, :, None] * bb_re[None] - rev[1][:, :, :, None] * bb_im[None]
    wb_im = rev[0][:, :, :, None] * bb_im[None] + rev[1][:, :, :, None] * bb_re[None]
    wb_re = jnp.transpose(wb_re, (1, 0, 3, 2)).reshape(g, c * p, n)
    wb_im = jnp.transpose(wb_im, (1, 0, 3, 2)).reshape(g, c * p, n)
    wc_re = jnp.transpose(cp_re[1:], (1, 3, 0, 2)).reshape(g, n, c * p)
    wc_im = -jnp.transpose(cp_im[1:], (1, 3, 0, 2)).reshape(g, n, c * p)

    def pair_diag(x):
        r, cc = x.shape[1:]
        x = x.reshape(g // 2, 2, r, cc)
        return jnp.einsum('kirc,ij->kirjc', x, jnp.eye(2, dtype=F32), precision=hp).reshape(g // 2, 2 * r, 2 * cc)

    d_chunk = jnp.broadcast_to(d.astype(F32).reshape(g, 1, p), (g, c, p)).reshape(1, g * c * p)

    gb, gl = g // SSM_BLOCK, SSM_BLOCK
    same = jnp.eye(gl, dtype=bool)
    m_src = jnp.transpose(m_intra.reshape(gb, gl, c, p, c, p), (0, 2, 1, 3, 4, 5))
    m_tok = jnp.where(same[None, None, :, None, None, :, None], m_src[:, :, :, :, :, None, :],
                      0.0).reshape(gb, gl * c * p, gl * c * p)

    def wb_tok(x):
        src = jnp.transpose(x.reshape(gb, gl, c, p, n), (0, 2, 1, 3, 4))
        return jnp.where(same[None, None, :, None, :, None], src[:, :, :, :, None, :],
                         0.0).reshape(gb, gl * c * p, gl * n)

    def wc_tok(x):
        src = x.reshape(gb, gl, n, c, p)
        return jnp.where(same[None, :, None, None, :, None], src[:, :, :, :, None, :],
                         0.0).reshape(gb, gl * n, gl * c * p)

    k8 = c * jnp.arange(1, SUBLANES + 1, dtype=F32)[:, None, None]
    mag8, ang8 = jnp.exp(a_re * dt * k8), a_im * dt * k8
    ap_re = jnp.transpose((mag8 * jnp.cos(ang8)).reshape(SUBLANES, gb, gl * n), (1, 0, 2))
    ap_im = jnp.transpose((mag8 * jnp.sin(ang8)).reshape(SUBLANES, gb, gl * n), (1, 0, 2))
    d_tok = jnp.broadcast_to(d.astype(F32).reshape(gb, 1, gl * p), (gb, c, gl * p)).reshape(gb, 1, c * gl * p)
    tok = dict(m=m_tok, wb_re=wb_tok(wb_re), wb_im=wb_tok(wb_im), wc_re=wc_tok(wc_re), wc_im=wc_tok(wc_im),
               ap_re=ap_re, ap_im=ap_im, d=d_tok)
    return dict(m=pair_diag(m_intra), wb_re=pair_diag(wb_re), wb_im=pair_diag(wb_im),
                wc_re=pair_diag(wc_re), wc_im=pair_diag(wc_im),
                a_re=pw_re[c].reshape(1, g * n), a_im=pw_im[c].reshape(1, g * n), d=d_chunk, tok=tok)


def _ssm_tok_kernel(u_ref, m_ref, wbre_ref, wbim_ref, wcre_ref, wcim_ref, apre_ref, apim_ref, d_ref,
                    z_ref, hre_ref, him_ref, gre_scr, gim_scr, *, nc):
    c = SSM_CHUNK
    x = jnp.concatenate([u_ref[pl.ds(s, nc, stride=c), :] for s in range(c)], axis=1)
    xb = x.astype(BF16)
    dot = functools.partial(jnp.dot, preferred_element_type=F32)
    g_re, g_im = dot(xb, wbre_ref[...]), dot(xb, wbim_ref[...])
    ap_re, ap_im = apre_ref[...], apim_ref[...]
    row = lax.broadcasted_iota(jnp.int32, g_re.shape, 0)
    sub = row % SUBLANES
    for k in (1, 2, 4):
        sh_re, sh_im = pltpu.roll(g_re, k, axis=0), pltpu.roll(g_im, k, axis=0)
        a_r, a_i = ap_re[k - 1:k], ap_im[k - 1:k]
        keep = sub >= k
        g_re, g_im = (g_re + jnp.where(keep, a_r * sh_re - a_i * sh_im, 0.0),
                      g_im + jnp.where(keep, a_r * sh_im + a_i * sh_re, 0.0))
    gre_scr[...] = g_re
    gim_scr[...] = g_im

    def tile(i, carry):
        c_re, c_im = carry
        r0 = pl.multiple_of(i * SUBLANES, SUBLANES)
        h_re = gre_scr[pl.ds(r0, SUBLANES), :] + (ap_re * c_re - ap_im * c_im)
        h_im = gim_scr[pl.ds(r0, SUBLANES), :] + (ap_re * c_im + ap_im * c_re)
        gre_scr[pl.ds(r0, SUBLANES), :] = h_re
        gim_scr[pl.ds(r0, SUBLANES), :] = h_im
        return h_re[SUBLANES - 1:], h_im[SUBLANES - 1:]

    zero = jnp.zeros((1, g_re.shape[1]), F32)
    f_re, f_im = lax.fori_loop(0, nc // SUBLANES, tile, (zero, zero))
    hre_ref[...] = f_re
    him_ref[...] = f_im
    e_re = jnp.where(row == 0, 0.0, pltpu.roll(gre_scr[...], 1, axis=0))
    e_im = jnp.where(row == 0, 0.0, pltpu.roll(gim_scr[...], 1, axis=0))
    y = (dot(xb, m_ref[...]) + dot(e_re.astype(BF16), wcre_ref[...]) + dot(e_im.astype(BF16), wcim_ref[...])
         + d_ref[...] * x)
    z = _gelu(y)
    for t in range(c):
        z_ref[pl.ds(t, nc, stride=c), :] = z[:, t * LANES:(t + 1) * LANES]


def _ssm_tok(u, w, bsz, seq):
    nc = seq // SSM_CHUNK
    gb = SSM_GROUPS // SSM_BLOCK
    sl = SSM_BLOCK * SSM_STATE
    wide = SSM_CHUNK * LANES
    tok_spec = pl.BlockSpec((seq, LANES), lambda k, b: (b, k))
    st_spec = pl.BlockSpec((None, 1, sl), lambda k, b: (b, 0, k))

    def wspec(r, cc):
        return pl.BlockSpec((None, r, cc), lambda k, b: (k, 0, 0))

    t = w['tok']
    return pl.pallas_call(
        functools.partial(_ssm_tok_kernel, nc=nc),
        grid=(gb, bsz),
        in_specs=[tok_spec, wspec(wide, wide), wspec(wide, sl), wspec(wide, sl), wspec(sl, wide), wspec(sl, wide),
                  wspec(SUBLANES, sl), wspec(SUBLANES, sl), wspec(1, wide)],
        out_specs=[tok_spec, st_spec, st_spec],
        out_shape=[jax.ShapeDtypeStruct((bsz * seq, SSM_WIDTH), F32),
                   jax.ShapeDtypeStruct((bsz, 1, SSM_GROUPS * SSM_STATE), F32),
                   jax.ShapeDtypeStruct((bsz, 1, SSM_GROUPS * SSM_STATE), F32)],
        scratch_shapes=[pltpu.VMEM((nc, sl), F32), pltpu.VMEM((nc, sl), F32)],
        compiler_params=_params("parallel", "parallel"),
        name="ssm_scan_tok",
    )(u, t['m'].astype(BF16), t['wb_re'].astype(BF16), t['wb_im'].astype(BF16),
      t['wc_re'].astype(BF16), t['wc_im'].astype(BF16), t['ap_re'], t['ap_im'], t['d'])


def _ssm_kernel(u_ref, h0re_ref, h0im_ref, m_ref, wbre_ref, wbim_ref, wcre_ref, wcim_ref,
                are_ref, aim_ref, d_ref, z_ref, hre_ref, him_ref,
                sre_scr, sim_scr, hsre_scr, hsim_scr, *, lc, bt, cdt, precision):
    ci = pl.program_id(2)
    rows = lc * bt
    sw = 2 * SSM_STATE
    dot = functools.partial(jnp.dot, preferred_element_type=F32, precision=precision)

    @pl.when(ci == 0)
    def _():
        hre_ref[...] = h0re_ref[...]
        him_ref[...] = h0im_ref[...]

    for k in range(SSM_PAIRS_PER_STEP):
        ug = u_ref[:, :, k * SSM_PAIR:(k + 1) * SSM_PAIR].reshape(rows, SSM_PAIR).astype(cdt)
        sre_scr[:, k * sw:(k + 1) * sw] = dot(ug, wbre_ref[k])
        sim_scr[:, k * sw:(k + 1) * sw] = dot(ug, wbim_ref[k])

    a_re = are_ref[...]
    a_im = aim_ref[...]

    def step(c, carry):
        h_re, h_im = carry
        r0 = pl.multiple_of(c * bt, bt)
        hsre_scr[pl.ds(r0, bt), :] = h_re
        hsim_scr[pl.ds(r0, bt), :] = h_im
        n_re = a_re * h_re - a_im * h_im + sre_scr[pl.ds(r0, bt), :]
        n_im = a_re * h_im + a_im * h_re + sim_scr[pl.ds(r0, bt), :]
        return n_re, n_im

    h_re, h_im = lax.fori_loop(0, lc, step, (hre_ref[...], him_ref[...]))
    hre_ref[...] = h_re
    him_ref[...] = h_im

    for k in range(SSM_PAIRS_PER_STEP):
        lanes = slice(k * SSM_PAIR, (k + 1) * SSM_PAIR)
        ug = u_ref[:, :, lanes].reshape(rows, SSM_PAIR)
        y = (dot(ug.astype(cdt), m_ref[k])
             + dot(hsre_scr[:, k * sw:(k + 1) * sw].astype(cdt), wcre_ref[k])
             + dot(hsim_scr[:, k * sw:(k + 1) * sw].astype(cdt), wcim_ref[k])
             + d_ref[:, lanes] * ug)
        z_ref[:, :, lanes] = _gelu(y).reshape(lc, bt, SSM_PAIR)


def _ssm(u_chunks, h0_re, h0_im, w, lc, bt, cdt, precision):
    nc, bsz, width = u_chunks.shape
    pps = SSM_PAIRS_PER_STEP
    sw = 2 * SSM_STATE
    u_spec = pl.BlockSpec((lc, bt, pps * SSM_PAIR), lambda p, b, c: (c, b, p))
    st_spec = pl.BlockSpec((bt, pps * sw), lambda p, b, c: (b, p))

    def wspec(r, cc):
        return pl.BlockSpec((pps, r, cc), lambda p, b, c: (p, 0, 0))

    return pl.pallas_call(
        functools.partial(_ssm_kernel, lc=lc, bt=bt, cdt=cdt, precision=precision),
        grid=(SSM_PAIRS // pps, bsz // bt, nc // lc),
        in_specs=[u_spec, st_spec, st_spec,
                  wspec(SSM_PAIR, SSM_PAIR), wspec(SSM_PAIR, sw), wspec(SSM_PAIR, sw),
                  wspec(sw, SSM_PAIR), wspec(sw, SSM_PAIR),
                  pl.BlockSpec((1, pps * sw), lambda p, b, c: (0, p)),
                  pl.BlockSpec((1, pps * sw), lambda p, b, c: (0, p)),
                  pl.BlockSpec((1, pps * SSM_PAIR), lambda p, b, c: (0, p))],
        out_specs=[u_spec, st_spec, st_spec],
        out_shape=[jax.ShapeDtypeStruct((nc, bsz, width), F32),
                   jax.ShapeDtypeStruct(h0_re.shape, F32),
                   jax.ShapeDtypeStruct(h0_im.shape, F32)],
        scratch_shapes=[pltpu.VMEM((lc * bt, pps * sw), F32) for _ in range(4)],
        compiler_params=_params("parallel", "parallel", "arbitrary"),
        name="ssm_scan",
    )(u_chunks, h0_re, h0_im, w['m'].astype(cdt), w['wb_re'].astype(cdt), w['wb_im'].astype(cdt),
      w['wc_re'].astype(cdt), w['wc_im'].astype(cdt), w['a_re'], w['a_im'], w['d'])


def _to_chunks(u, bsz, t):
    nc = t // SSM_CHUNK
    u = u.reshape(bsz, nc, SSM_CHUNK, SSM_GROUPS, SSM_GROUP)
    return jnp.transpose(u, (1, 0, 3, 2, 4)).reshape(nc, bsz, SSM_GROUPS * SSM_CHUNK * SSM_GROUP)


def _from_chunks(z, bsz, t):
    nc = t // SSM_CHUNK
    z = z.reshape(nc, bsz, SSM_GROUPS, SSM_CHUNK, SSM_GROUP)
    return jnp.transpose(z, (1, 0, 3, 2, 4)).reshape(bsz * t, SSM_WIDTH)


def _glu_kernel(z_ref, w_ref, b_ref, o_ref):
    z = z_ref[...]
    gate = jnp.dot(z.astype(BF16), w_ref[...], preferred_element_type=F32) + b_ref[...]
    o_ref[...] = (z * _sigmoid(gate)).astype(o_ref.dtype)


def _glu(z, w_bf16, b, tm=512):
    m = z.shape[0]
    return pl.pallas_call(
        _glu_kernel,
        grid=(m // tm,),
        in_specs=[pl.BlockSpec((tm, SSM_WIDTH), lambda i: (i, 0)),
                  pl.BlockSpec((SSM_WIDTH, SSM_WIDTH), lambda i: (0, 0)),
                  pl.BlockSpec((1, SSM_WIDTH), lambda i: (0, 0))],
        out_specs=pl.BlockSpec((tm, SSM_WIDTH), lambda i: (i, 0)),
        out_shape=jax.ShapeDtypeStruct((m, SSM_WIDTH), BF16),
        compiler_params=_params("parallel"),
        name="ssm_glu",
    )(z, w_bf16, b.reshape(1, SSM_WIDTH))


def _out_proj_kernel(attn_ref, ssm_ref, x_ref, w_ref, g_ref, h_ref, xn_ref):
    mix = (jnp.dot(attn_ref[...].astype(BF16), w_ref[:ATTN_WIDTH, :], preferred_element_type=F32)
           + jnp.dot(ssm_ref[...], w_ref[ATTN_WIDTH:, :], preferred_element_type=F32))
    h = x_ref[...] + mix
    h_ref[...] = h
    xn_ref[...] = _rms(h, g_ref[...]).astype(BF16)


def _out_proj(attn, ssm_out, x, w_bf16, g, tm=256):
    m = x.shape[0]
    return pl.pallas_call(
        _out_proj_kernel,
        grid=(m // tm,),
        in_specs=[pl.BlockSpec((tm, ATTN_WIDTH), lambda i: (i, 0)),
                  pl.BlockSpec((tm, SSM_WIDTH), lambda i: (i, 0)),
                  pl.BlockSpec((tm, D_MODEL), lambda i: (i, 0)),
                  pl.BlockSpec((D_MODEL, D_MODEL), lambda i: (0, 0)),
                  pl.BlockSpec((1, D_MODEL), lambda i: (0, 0))],
        out_specs=[pl.BlockSpec((tm, D_MODEL), lambda i: (i, 0)),
                   pl.BlockSpec((tm, D_MODEL), lambda i: (i, 0))],
        out_shape=[jax.ShapeDtypeStruct((m, D_MODEL), F32),
                   jax.ShapeDtypeStruct((m, D_MODEL), BF16)],
        compiler_params=_params("parallel"),
        name="out_proj",
    )(attn, ssm_out, x, w_bf16, g.reshape(1, D_MODEL))


def _pick16(s, exact):
    iota = lax.broadcasted_iota(jnp.int32, s.shape, 0)
    rank = jnp.full(s.shape, 31, jnp.int32)
    vals = []
    for r in range(PEER_TOPK):
        mx = jnp.max(s, axis=0, keepdims=True)
        hit = s == mx
        if exact:
            hit = iota == jnp.min(jnp.where(hit, iota, s.shape[0]), axis=0, keepdims=True)
        rank = jnp.where(hit, r, rank)
        s = jnp.where(hit, -jnp.inf, s)
        vals.append(mx)
    return vals, rank


def _peer_route_kernel(xn_ref, wq_ref, keys_ref, lena_ref, rkb_ref, ea_ref, eb_ref,
                       q_scr, cand_scr, sel_scr):
    tm = xn_ref.shape[0]
    q_scr[...] = jnp.dot(xn_ref[...], wq_ref[...], preferred_element_type=F32).astype(BF16)
    cand_scr[...] = jnp.full_like(cand_scr, -jnp.inf)
    nt = (((1,), (1,)), ((), ()))

    def head(idx, _):
        h = idx // (tm // LANES)
        sub = idx % (tm // LANES)
        r0 = pl.multiple_of(sub * LANES, LANES)
        c0 = pl.multiple_of(h * 2 * PEER_HALF, 2 * PEER_HALF)
        qa = q_scr[pl.ds(r0, LANES), pl.ds(c0, PEER_HALF)]
        qb = q_scr[pl.ds(r0, LANES), pl.ds(c0 + PEER_HALF, PEER_HALF)]
        s_a = lax.dot_general(keys_ref[0], qa, nt, preferred_element_type=F32)
        s_b = lax.dot_general(keys_ref[1], qb, nt, preferred_element_type=F32)

        def route(exact):
            va, ra = _pick16(s_a, exact)
            vb, rb = _pick16(s_b, exact)
            for p, (i, j) in enumerate(_PEER_CAND):
                cand_scr[p:p + 1, :] = va[i] + vb[j]
            cand = cand_scr[...]
            _, rc = _pick16(cand, exact)
            sel = jnp.where(rc < 31, 1.0, 0.0)
            top = va[0] + vb[0]
            z = jnp.sum(sel * jnp.exp(cand - top), axis=0, keepdims=True)
            sel_scr[...] = sel
            lena = jnp.zeros(s_a.shape, F32)
            p = 0
            for i in range(PEER_TOPK):
                n_i = PEER_TOPK // (i + 1)
                len_i = jnp.sum(sel_scr[p:p + n_i, :], axis=0, keepdims=True)
                lena = jnp.where(ra == i, len_i, lena)
                p += n_i
            lena_ref[h, :, pl.ds(r0, LANES)] = lena
            rkb_ref[h, :, pl.ds(r0, LANES)] = rb.astype(F32).astype(BF16)
            ea_ref[h, :, pl.ds(r0, LANES)] = jnp.exp(s_a - va[0]) * (1.0 / z)
            eb_ref[h, :, pl.ds(r0, LANES)] = jnp.exp(s_b - vb[0]).astype(BF16)
            ranked = (jnp.sum(jnp.where(ra < 31, 1.0, 0.0), axis=0, keepdims=True)
                      + jnp.sum(jnp.where(rb < 31, 1.0, 0.0), axis=0, keepdims=True)
                      + jnp.sum(sel, axis=0, keepdims=True))
            return jnp.max(ranked)

        ranked = route(exact=False)

        @pl.when(ranked > 3 * PEER_TOPK)
        def _():
            route(exact=True)

        return 0

    lax.fori_loop(0, PEER_HEADS * (tm // LANES), head, 0)


def _peer_route(xn, wq_bf16, keys_bf16, tm=512):
    m = xn.shape[0]
    out_spec = pl.BlockSpec((PEER_HEADS, PEER_NKEYS, tm), lambda i: (0, 0, i))
    out_sds = [jax.ShapeDtypeStruct((PEER_HEADS, PEER_NKEYS, m), dt) for dt in (F32, BF16, F32, BF16)]
    return pl.pallas_call(
        _peer_route_kernel,
        grid=(m // tm,),
        in_specs=[pl.BlockSpec((tm, D_MODEL), lambda i: (i, 0)),
                  pl.BlockSpec((D_MODEL, D_MODEL), lambda i: (0, 0)),
                  pl.BlockSpec((2, PEER_NKEYS, PEER_HALF), lambda i: (0, 0, 0))],
        out_specs=[out_spec] * 4,
        out_shape=out_sds,
        scratch_shapes=[pltpu.VMEM((tm, D_MODEL), BF16),
                        pltpu.VMEM((_PEER_CAND_ROWS, LANES), F32),
                        pltpu.VMEM((_PEER_CAND_ROWS, LANES), F32)],
        compiler_params=_params("parallel"),
        name="peer_route",
    )(xn, wq_bf16, keys_bf16)


def _peer_dense_kernel(xn_ref, u_ref, v_ref, lena_ref, rkb_ref, ea_ref, eb_ref, o_ref,
                       pre0, pre1, ht0, ht1, *, ta, nj):
    j = pl.program_id(1)

    @pl.when(j == 0)
    def _():
        for ref in (pre0, pre1, ht0, ht1, o_ref):
            ref[...] = jnp.zeros_like(ref)

    def step(pre_w, pre_r, ht_w, ht_r):
        o_ref[...] += lax.dot_general(ht_r[...], v_ref[...], (((0,), (0,)), ((), ())),
                                      preferred_element_type=F32)
        tile = jnp.clip(j - 1, 0, nj - 1)
        for t in range(ta):
            a = tile * ta + t
            rows = slice(t * PEER_NKEYS, (t + 1) * PEER_NKEYS)
            w = None
            for h in range(PEER_HEADS):
                lena = lena_ref[h, pl.ds(a, 1), :].astype(BF16)
                ea = ea_ref[h, pl.ds(a, 1), :].astype(BF16)
                term = jnp.where(rkb_ref[h] < lena, eb_ref[h], jnp.zeros((), BF16)) * ea
                w = term if w is None else w + term
            ht_w[rows, :] = w * _gelu(pre_r[rows, :]).astype(BF16)
        pre_w[...] = lax.dot_general(u_ref[...], xn_ref[...], (((1,), (1,)), ((), ())),
                                     preferred_element_type=F32)

    @pl.when(j % 2 == 0)
    def _():
        step(pre0, pre1, ht1, ht0)

    @pl.when(j % 2 == 1)
    def _():
        step(pre1, pre0, ht0, ht1)


def _peer_dense(xn, u_bf16, v_bf16, route, tm=512, ta=4):
    m = xn.shape[0]
    nj = PEER_NKEYS // ta
    r_spec = pl.BlockSpec((PEER_HEADS, PEER_NKEYS, tm), lambda i, j: (0, 0, i))
    rows = ta * PEER_NKEYS
    return pl.pallas_call(
        functools.partial(_peer_dense_kernel, ta=ta, nj=nj),
        grid=(m // tm, nj + 2),
        in_specs=[pl.BlockSpec((tm, D_MODEL), lambda i, j: (i, 0)),
                  pl.BlockSpec((rows, D_MODEL), lambda i, j: (jnp.minimum(j, nj - 1), 0)),
                  pl.BlockSpec((rows, D_MODEL), lambda i, j: (jnp.clip(j - 2, 0, nj - 1), 0)),
                  r_spec, r_spec, r_spec, r_spec],
        out_specs=pl.BlockSpec((tm, D_MODEL), lambda i, j: (i, 0)),
        out_shape=jax.ShapeDtypeStruct((m, D_MODEL), F32),
        scratch_shapes=[pltpu.VMEM((rows, tm), F32), pltpu.VMEM((rows, tm), F32),
                        pltpu.VMEM((rows, tm), BF16), pltpu.VMEM((rows, tm), BF16)],
        compiler_params=_params("parallel", "arbitrary"),
        name="peer_dense",
    )(xn, u_bf16, v_bf16, *route)


def _ple_kernel(h_ref, peer_ref, ple_ref, pw_ref, gw_ref, g_ref, o_ref, *, final_norm):
    h = h_ref[...] + peer_ref[...]
    gate = _sigmoid(jnp.dot(h.astype(BF16), gw_ref[...], preferred_element_type=F32))
    emb = jnp.dot(ple_ref[...].astype(BF16), pw_ref[...], preferred_element_type=F32)
    h = h + emb * gate
    if final_norm:
        h = _rms(h, g_ref[...])
    o_ref[...] = h


def _ple(h, peer_out, ple, pw_bf16, gw_bf16, g, final_norm, tm=256):
    m = h.shape[0]
    row = pl.BlockSpec((tm, D_MODEL), lambda i: (i, 0))
    return pl.pallas_call(
        functools.partial(_ple_kernel, final_norm=final_norm),
        grid=(m // tm,),
        in_specs=[row, row,
                  pl.BlockSpec((tm, PLE_DIM), lambda i: (i, 0)),
                  pl.BlockSpec((PLE_DIM, D_MODEL), lambda i: (0, 0)),
                  pl.BlockSpec((D_MODEL, D_MODEL), lambda i: (0, 0)),
                  pl.BlockSpec((1, D_MODEL), lambda i: (0, 0))],
        out_specs=row,
        out_shape=jax.ShapeDtypeStruct((m, D_MODEL), F32),
        compiler_params=_params("parallel"),
        name="ple_gate",
    )(h, peer_out, ple, pw_bf16, gw_bf16, g.reshape(1, D_MODEL))


def kernel(x_prompt, x_sample, cache_k, cache_v, state_ssm_re, state_ssm_im, page_table, p_prompt, p_sample, norm_mix_g, norm_ffn_g, w_in, w_out, lam_q1, lam_k1, lam_q2, lam_k2, subln_g, rel_bias, ssm_a_re, ssm_a_im, ssm_b_re, ssm_b_im, ssm_c_re, ssm_c_im, ssm_d, ssm_log_dt, ssm_w_glu, ssm_b_glu, peer_w_q, peer_keys, peer_u, peer_v, ple_w, ple_gate_w, final_norm_g):
    bsz, seq = x_prompt.shape[:2]
    dec_batch, dec_seq = x_sample.shape[:2]
    n_pool = cache_k.shape[1]
    n_pages = page_table.shape[1]
    tq = 256
    nq = seq // tq

    bias_prompt = _prompt_bias_tiles(rel_bias, nq, tq)
    bias_pages, bias_new = _sample_bias_tables(rel_bias, n_pages, dec_seq)

    cache_k = cache_k.reshape(DEPTH, n_pool, PAGE_ROWS * N_HEADS, VD)
    cache_v = cache_v.reshape(DEPTH, n_pool, PAGE_ROWS * N_HEADS, VD)
    zeros_state = jnp.zeros((bsz, SSM_GROUPS * SSM_STATE), F32)

    hp = x_prompt.reshape(bsz * seq, D_MODEL)
    hs = x_sample.reshape(dec_batch * dec_seq, D_MODEL)
    outs = {k: [] for k in ('kp', 'vp', 'spr', 'spi', 'ks', 'vs', 'ssr', 'ssi')}

    for i in range(DEPTH):
        lam_init = 0.8 - 0.6 * math.exp(-0.3 * i)
        lam_params = jnp.stack([lam_q1[i], lam_k1[i], lam_q2[i], lam_k2[i]]).astype(F32)
        w_in_b = w_in[i].astype(BF16)
        w_out_b = w_out[i].astype(BF16)
        w_glu_b = ssm_w_glu[i].astype(BF16)
        wq_b = peer_w_q[i].astype(BF16)
        keys_b = peer_keys[i].astype(BF16)
        pu_b = peer_u[i].astype(BF16)
        pv_b = peer_v[i].astype(BF16)
        plw_b = ple_w[i].astype(BF16)
        gw_b = ple_gate_w[i].astype(BF16)
        ssm_w = _ssm_weights(ssm_a_re[i], ssm_a_im[i], ssm_b_re[i], ssm_b_im[i], ssm_c_re[i], ssm_c_im[i],
                             ssm_d[i], ssm_log_dt[i])
        last = i == DEPTH - 1

        def tail(h_in, attn, ssm_z, ple):
            ssm_out = _glu(ssm_z, w_glu_b, ssm_b_glu[i])
            h_mid, xn = _out_proj(attn, ssm_out, h_in, w_out_b, norm_ffn_g[i])
            route = _peer_route(xn, wq_b, keys_b)
            peer_out = _peer_dense(xn, pu_b, pv_b, route)
            return _ple(h_mid, peer_out, ple, plw_b, gw_b, final_norm_g, last)

        q, k_new, v_new, u = _in_proj(hp, norm_mix_g[i], w_in_b, BF16)
        attn = _attn_prompt(q, k_new, v_new, bias_prompt, lam_params, subln_g[i], lam_init, bsz, seq, tq)
        z, s_re, s_im = _ssm_tok(u, ssm_w, bsz, seq)
        hp = tail(hp, attn, z, p_prompt[i].reshape(bsz * seq, PLE_DIM))
        outs['kp'].append(k_new.reshape(bsz, seq, N_HEADS, VD))
        outs['vp'].append(v_new.reshape(bsz, seq, N_HEADS, VD))
        outs['spr'].append(s_re.reshape(bsz, SSM_GROUPS, SSM_STATE))
        outs['spi'].append(s_im.reshape(bsz, SSM_GROUPS, SSM_STATE))

        q, k_new, v_new, u = _in_proj(hs, norm_mix_g[i], w_in_b, F32)
        attn = _attn_sample(i, page_table, q, k_new, v_new, cache_k, cache_v, bias_pages, bias_new,
                            lam_params, subln_g[i], lam_init, dec_batch, dec_seq)
        z, s_re, s_im = _ssm(_to_chunks(u, dec_batch, dec_seq),
                             state_ssm_re[i].astype(F32).reshape(dec_batch, -1),
                             state_ssm_im[i].astype(F32).reshape(dec_batch, -1), ssm_w,
                             lc=1, bt=dec_batch, cdt=F32, precision=lax.Precision.HIGHEST)
        hs = tail(hs, attn, _from_chunks(z, dec_batch, dec_seq), p_sample[i].reshape(dec_batch * dec_seq, PLE_DIM))
        outs['ks'].append(k_new.reshape(dec_batch, dec_seq, N_HEADS, VD))
        outs['vs'].append(v_new.reshape(dec_batch, dec_seq, N_HEADS, VD))
        outs['ssr'].append(s_re.reshape(dec_batch, SSM_GROUPS, SSM_STATE))
        outs['ssi'].append(s_im.reshape(dec_batch, SSM_GROUPS, SSM_STATE))

    return (hp.reshape(bsz, seq, D_MODEL), hs.reshape(dec_batch, dec_seq, D_MODEL),
            jnp.stack(outs['kp']), jnp.stack(outs['vp']), jnp.stack(outs['spr']), jnp.stack(outs['spi']),
            jnp.stack(outs['ks']), jnp.stack(outs['vs']), jnp.stack(outs['ssr']), jnp.stack(outs['ssi']))
```
